```python
import math
import jax, jax.numpy as jnp
from jax import lax
import numpy as np

D_MODEL = 1024
BATCH = 16
SEQ = 2048
DEPTH = 2

RWKV_HEADS = 8
RWKV_HEAD_DIM = 64
RWKV_WIDTH = RWKV_HEADS * RWKV_HEAD_DIM
RWKV_DECAY_RANK = 64
RWKV_ICLR_RANK = 64
RWKV_GATE_RANK = 128
RWKV_SHIFT_COLS = 3 * RWKV_WIDTH + RWKV_DECAY_RANK + RWKV_ICLR_RANK + RWKV_GATE_RANK
RWKV_LN_EPS = 64e-5
SSD_HEADS = 8
SSD_HEAD_DIM = 64
SSD_WIDTH = SSD_HEADS * SSD_HEAD_DIM
SSD_GROUPS = 2
SSD_STATE = 128
SSD_CONV = 4
SSD_CHUNK = 128
SSD_XBC = SSD_WIDTH + 2 * SSD_GROUPS * SSD_STATE
SSD_NORM_EPS = 1e-5
S5_GROUP = 16
S5_GROUPS = 32
S5_WIDTH = S5_GROUP * S5_GROUPS
S5_STATE = 64
N_BRANCHES = 3
FFN_HIDDEN = -(-8 * D_MODEL // (3 * 256)) * 256
NORM_EPS = 1e-6
IN_COLS = RWKV_SHIFT_COLS + SSD_WIDTH + SSD_XBC + SSD_HEADS + S5_WIDTH + N_BRANCHES * D_MODEL

kernel_name = "hybrid_rwkv7_ssd_s5_gated_trunk"


def _rms_norm(x, g, eps=NORM_EPS):
    xf = x.astype(jnp.float32)
    y = xf * lax.rsqrt(jnp.mean(xf * xf, axis=-1, keepdims=True) + eps)
    return (y * g.astype(jnp.float32)).astype(x.dtype)


def _token_shift(p):
    return jnp.pad(p, ((0, 0), (1, 0), (0, 0)))[:, :-1]


def _rwkv7_mixer(p, mu, w0, w_up, a0, a_up, g_up, k_k, k_a, r_k, ln_g, ln_b):
    f32 = jnp.float32
    bsz, T, _ = p.shape
    H, N, W = RWKV_HEADS, RWKV_HEAD_DIM, RWKV_WIDTH
    p = p + (_token_shift(p) - p) * mu
    r, k, v, wd, ad, gd = jnp.split(
        p, [W, 2 * W, 3 * W, 3 * W + RWKV_DECAY_RANK, 3 * W + RWKV_DECAY_RANK + RWKV_ICLR_RANK], axis=-1)
    w_log = -jax.nn.softplus(-(w0 + jnp.tanh(wd) @ w_up).astype(f32)) - 0.5
    decay = jnp.exp(-jnp.exp(w_log))
    a = jax.nn.sigmoid((a0 + ad @ a_up).astype(f32))
    g = jax.nn.sigmoid(gd) @ g_up
    heads = lambda t: t.reshape(bsz, T, H, N)
    kk = heads(k * k_k).astype(f32)
    kk = kk / jnp.maximum(jnp.linalg.norm(kk, axis=-1, keepdims=True), 1e-12)
    k = k.astype(f32) * (1.0 + (a - 1.0) * k_a)
    rh, kh, vh, wh, ah = heads(r.astype(f32)), heads(k), heads(v.astype(f32)), heads(decay), heads(a)

    def step(S, inp):
        r_t, w_t, k_t, v_t, kk_t, a_t = inp
        sa = jnp.einsum('bhvk,bhk->bhv', S, -kk_t)
        S = (S * w_t[:, :, None, :] + sa[..., None] * (kk_t * a_t)[:, :, None, :]
             + v_t[..., None] * k_t[:, :, None, :])
        return S, jnp.einsum('bhvk,bhk->bhv', S, r_t)

    tm = lambda t: jnp.moveaxis(t, 1, 0)
    S0 = jnp.zeros((bsz, H, N, N), f32)
    _, o = lax.scan(step, S0, (tm(rh), tm(wh), tm(kh), tm(vh), tm(kk), tm(ah)))
    o = jnp.moveaxis(o, 0, 1)
    mean = jnp.mean(o, axis=-1, keepdims=True)
    var = jnp.var(o, axis=-1, keepdims=True)
    o = ((o - mean) * lax.rsqrt(var + RWKV_LN_EPS)).reshape(bsz, T, W) * ln_g + ln_b
    bonus = jnp.sum(rh * kh * r_k, axis=-1, keepdims=True) * vh
    o = o + bonus.reshape(bsz, T, W)
    return (o * g).astype(p.dtype)


def _segsum(x):
    L = x.shape[-1]
    cs = jnp.cumsum(x, axis=-1)
    diff = cs[..., :, None] - cs[..., None, :]
    mask = jnp.tril(jnp.ones((L, L), dtype=bool))
    return jnp.where(mask, diff, -jnp.inf)


def _ssd_chunked(x, dA, Bm, Cm):
    b, T, H, P = x.shape
    G, N = Bm.shape[2], Bm.shape[3]
    J, L = H // G, SSD_CHUNK
    nc = T // L
    xc = x.reshape(b, nc, L, G, J, P)
    Bc = Bm.reshape(b, nc, L, G, N)
    Cc = Cm.reshape(b, nc, L, G, N)
    A = dA.reshape(b, nc, L, G, J).transpose(0, 3, 4, 1, 2)
    A_cs = jnp.cumsum(A, axis=-1)
    Lmat = jnp.exp(_segsum(A))
    scores = jnp.einsum('bclgn,bcsgn->bgcls', Cc, Bc)
    y_diag = jnp.einsum('bgjcls,bcsgjp->bclgjp', scores[:, :, None] * Lmat, xc)
    decay_states = jnp.exp(A_cs[..., -1:] - A_cs)
    states = jnp.einsum('bclgn,bgjcl,bclgjp->bcgjpn', Bc, decay_states, xc)
    states = jnp.concatenate([jnp.zeros_like(states[:, :1]), states], axis=1)
    chunk_decay = jnp.exp(_segsum(jnp.pad(A_cs[..., -1], ((0, 0), (0, 0), (0, 0), (1, 0)))))
    states = jnp.einsum('bgjzc,bcgjpn->bzgjpn', chunk_decay, states)[:, :-1]
    y_off = jnp.einsum('bclgn,bcgjpn,bgjcl->bclgjp', Cc, states, jnp.exp(A_cs))
    return (y_diag + y_off).reshape(b, T, H, P)


def _mamba2_mixer(z, xbc, dt_raw, conv_w, conv_b, dt_bias, a_log, d_skip, norm_g):
    f32 = jnp.float32
    bsz, T, _ = z.shape
    xbc = lax.conv_general_dilated(
        xbc, conv_w[:, None, :], window_strides=(1,), padding=[(SSD_CONV - 1, 0)],
        dimension_numbers=('NWC', 'WIO', 'NWC'), feature_group_count=SSD_XBC) + conv_b
    xbc = jax.nn.silu(xbc)
    xs, Bm, Cm = jnp.split(xbc, [SSD_WIDTH, SSD_WIDTH + SSD_GROUPS * SSD_STATE], axis=-1)
    xs = xs.astype(f32).reshape(bsz, T, SSD_HEADS, SSD_HEAD_DIM)
    Bm = Bm.astype(f32).reshape(bsz, T, SSD_GROUPS, SSD_STATE)
    Cm = Cm.astype(f32).reshape(bsz, T, SSD_GROUPS, SSD_STATE)
    dt = jax.nn.softplus((dt_raw + dt_bias).astype(f32))
    A = -jnp.exp(a_log.astype(f32))
    y = _ssd_chunked(xs * dt[..., None], dt * A, Bm, Cm) + xs * d_skip.astype(f32)[:, None]
    y = y.reshape(bsz, T, SSD_WIDTH) * jax.nn.silu(z.astype(f32))
    yg = y.reshape(bsz, T, SSD_GROUPS, SSD_WIDTH // SSD_GROUPS)
    yg = yg * lax.rsqrt(jnp.mean(yg * yg, axis=-1, keepdims=True) + SSD_NORM_EPS)
    return (yg.reshape(bsz, T, SSD_WIDTH) * norm_g).astype(z.dtype)


def _complex_affine_combine(e1, e2):
    a1r, a1i, b1r, b1i = e1
    a2r, a2i, b2r, b2i = e2
    return (a2r * a1r - a2i * a1i, a2r * a1i + a2i * a1r,
            a2r * b1r - a2i * b1i + b2r, a2r * b1i + a2i * b1r + b2i)


def _s5_mixer(u, lam_re, lam_im, log_step, b_re, b_im, c_re, c_im, d_skip, glu_w, glu_b):
    f32 = jnp.float32
    bsz, T, _ = u.shape
    G, N = S5_GROUPS, S5_STATE
    uf = u.astype(f32)
    lr, li = lam_re.astype(f32), lam_im.astype(f32)
    step = jnp.exp(log_step.astype(f32))[:, None]
    mag = jnp.exp(lr * step)
    ab_re, ab_im = mag * jnp.cos(li * step), mag * jnp.sin(li * step)
    den = lr * lr + li * li
    coef_re = ((ab_re - 1.0) * lr + ab_im * li) / den
    coef_im = (ab_im * lr - (ab_re - 1.0) * li) / den
    br, bi = b_re.astype(f32), b_im.astype(f32)
    bb_re = coef_re[..., None] * br - coef_im[..., None] * bi
    bb_im = coef_re[..., None] * bi + coef_im[..., None] * br
    ug = uf.reshape(bsz, T, G, S5_GROUP)
    bu_re = jnp.einsum('btgi,gni->btgn', ug, bb_re)
    bu_im = jnp.einsum('btgi,gni->btgn', ug, bb_im)
    a_re = jnp.broadcast_to(ab_re, (1, T, G, N))
    a_im = jnp.broadcast_to(ab_im, (1, T, G, N))
    _, _, s_re, s_im = lax.associative_scan(_complex_affine_combine, (a_re, a_im, bu_re, bu_im), axis=1)
    y = (jnp.einsum('btgn,gon->btgo', s_re, c_re.astype(f32))
         - jnp.einsum('btgn,gon->btgo', s_im, c_im.astype(f32)))
    y = y.reshape(bsz, T, S5_WIDTH) + d_skip.astype(f32) * uf
    zg = jax.nn.gelu(y)
    out = zg * jax.nn.sigmoid(zg @ glu_w.astype(f32) + glu_b.astype(f32))
    return out.astype(u.dtype)


def setup_inputs(seed: int = 0) -> dict:
    key = jax.random.key(seed)
    f32 = jnp.float32
    L = DEPTH

    def nrm(i, shape, scale=1.0):
        return jax.random.normal(jax.random.fold_in(key, i), shape, f32) * scale

    def unif(i, shape, lo, hi):
        return jax.random.uniform(jax.random.fold_in(key, i), shape, f32, lo, hi)

    ssd_dt = jnp.exp(unif(17, (L, SSD_HEADS), math.log(1e-3), math.log(1e-1)))
    return {
        "x": nrm(0, (BATCH, SEQ, D_MODEL)),
        "norm_mix": 1.0 + nrm(1, (L, D_MODEL), 0.02),
        "w_in": nrm(2, (L, D_MODEL, IN_COLS), D_MODEL ** -0.5),
        "rwkv_mu": unif(3, (L, RWKV_SHIFT_COLS), 0.0, 1.0),
        "rwkv_w0": jnp.linspace(-6.0, -1.0, RWKV_WIDTH, dtype=f32) + nrm(4, (L, RWKV_WIDTH), 0.1),
        "rwkv_w_up": nrm(5, (L, RWKV_DECAY_RANK, RWKV_WIDTH), 0.1 * RWKV_DECAY_RANK ** -0.5),
        "rwkv_a0": nrm(6, (L, RWKV_WIDTH), 0.1),
        "rwkv_a_up": nrm(7, (L, RWKV_ICLR_RANK, RWKV_WIDTH), 0.1 * RWKV_ICLR_RANK ** -0.5),
        "rwkv_g_up": nrm(8, (L, RWKV_GATE_RANK, RWKV_WIDTH), RWKV_GATE_RANK ** -0.5),
        "rwkv_k_k": 0.85 + nrm(9, (L, RWKV_WIDTH), 0.05),
        "rwkv_k_a": 1.0 + nrm(10, (L, RWKV_WIDTH), 0.05),
        "rwkv_r_k": nrm(11, (L, RWKV_HEADS, RWKV_HEAD_DIM), 0.1),
        "rwkv_ln_g": 1.0 + nrm(12, (L, RWKV_WIDTH), 0.02),
        "rwkv_ln_b": nrm(13, (L, RWKV_WIDTH), 0.02),
        "proj_a": nrm(14, (L, RWKV_WIDTH, D_MODEL), RWKV_WIDTH ** -0.5),
        "ssd_conv_w": nrm(15, (L, SSD_CONV, SSD_XBC), SSD_CONV ** -0.5),
        "ssd_conv_b": nrm(16, (L, SSD_XBC), 0.02),
        "ssd_dt_bias": ssd_dt + jnp.log(-jnp.expm1(-ssd_dt)),
        "ssd_a_log": jnp.log(unif(18, (L, SSD_HEADS), 1.0, 16.0)),
        "ssd_d": 1.0 + nrm(19, (L, SSD_HEADS), 0.02),
        "ssd_norm_g": 1.0 + nrm(20, (L, SSD_WIDTH), 0.02),
        "proj_b": nrm(21, (L, SSD_WIDTH, D_MODEL), SSD_WIDTH ** -0.5),
        "s5_lam_re": -0.5 + nrm(22, (L, S5_GROUPS, S5_STATE), 0.01),
        "s5_lam_im": math.pi * jnp.arange(S5_STATE, dtype=f32) + nrm(23, (L, S5_GROUPS, S5_STATE), 0.01),
        "s5_log_step": unif(24, (L, S5_GROUPS), math.log(1e-3), math.log(1e-1)),
        "s5_b_re": nrm(25, (L, S5_GROUPS, S5_STATE, S5_GROUP), (2.0 * S5_GROUP) ** -0.5),
        "s5_b_im": nrm(26, (L, S5_GROUPS, S5_STATE, S5_GROUP), (2.0 * S5_GROUP) ** -0.5),
        "s5_c_re": nrm(27, (L, S5_GROUPS, S5_GROUP, S5_STATE), S5_STATE ** -0.5),
        "s5_c_im": nrm(28, (L, S5_GROUPS, S5_GROUP, S5_STATE), S5_STATE ** -0.5),
        "s5_d": nrm(29, (L, S5_WIDTH)),
        "s5_glu_w": nrm(30, (L, S5_WIDTH, S5_WIDTH), S5_WIDTH ** -0.5),
        "s5_glu_b": nrm(31, (L, S5_WIDTH), 0.02),
        "proj_c": nrm(32, (L, S5_WIDTH, D_MODEL), S5_WIDTH ** -0.5),
        "w_out": nrm(33, (L, D_MODEL, D_MODEL), D_MODEL ** -0.5),
        "norm_ffn": 1.0 + nrm(34, (L, D_MODEL), 0.02),
        "ffn_w_in": nrm(35, (L, D_MODEL, 2 * FFN_HIDDEN), D_MODEL ** -0.5),
        "ffn_w_out": nrm(36, (L, FFN_HIDDEN, D_MODEL), FFN_HIDDEN ** -0.5),
        "final_norm": 1.0 + nrm(37, (D_MODEL,), 0.02),
    }


def reference(x, norm_mix, w_in, rwkv_mu, rwkv_w0, rwkv_w_up, rwkv_a0, rwkv_a_up, rwkv_g_up,
              rwkv_k_k, rwkv_k_a, rwkv_r_k, rwkv_ln_g, rwkv_ln_b, proj_a,
              ssd_conv_w, ssd_conv_b, ssd_dt_bias, ssd_a_log, ssd_d, ssd_norm_g, proj_b,
              s5_lam_re, s5_lam_im, s5_log_step, s5_b_re, s5_b_im, s5_c_re, s5_c_im, s5_d,
              s5_glu_w, s5_glu_b, proj_c, w_out, norm_ffn, ffn_w_in, ffn_w_out, final_norm):
    bsz, T, _ = x.shape
    o1 = RWKV_SHIFT_COLS
    o2 = o1 + SSD_WIDTH
    o3 = o2 + SSD_XBC
    o4 = o3 + SSD_HEADS
    o5 = o4 + S5_WIDTH
    for l in range(DEPTH):
        h = _rms_norm(x, norm_mix[l])
        proj = h @ w_in[l]
        p_rwkv, z, xbc, dt_raw, u, gate_logits = jnp.split(proj, [o1, o2, o3, o4, o5], axis=-1)
        y_a = _rwkv7_mixer(p_rwkv, rwkv_mu[l], rwkv_w0[l], rwkv_w_up[l], rwkv_a0[l], rwkv_a_up[l],
                           rwkv_g_up[l], rwkv_k_k[l], rwkv_k_a[l], rwkv_r_k[l],
                           rwkv_ln_g[l], rwkv_ln_b[l]) @ proj_a[l]
        y_b = _mamba2_mixer(z, xbc, dt_raw, ssd_conv_w[l], ssd_conv_b[l], ssd_dt_bias[l],
                            ssd_a_log[l], ssd_d[l], ssd_norm_g[l]) @ proj_b[l]
        y_c = _s5_mixer(u, s5_lam_re[l], s5_lam_im[l], s5_log_step[l], s5_b_re[l], s5_b_im[l],
                        s5_c_re[l], s5_c_im[l], s5_d[l], s5_glu_w[l], s5_glu_b[l]) @ proj_c[l]
        gates = jax.nn.sigmoid(gate_logits.astype(jnp.float32)).reshape(bsz, T, N_BRANCHES, D_MODEL)
        gates = gates.astype(x.dtype)
        merged = gates[:, :, 0] * y_a + gates[:, :, 1] * y_b + gates[:, :, 2] * y_c
        x = x + merged @ w_out[l]
        hf = _rms_norm(x, norm_ffn[l])
        gate, up = jnp.split(hf @ ffn_w_in[l], 2, axis=-1)
        x = x + (jax.nn.silu(gate) * up) @ ffn_w_out[l]
    return _rms_norm(x, final_norm)
```

```python
import functools

import jax
import jax.numpy as jnp
from jax import lax
from jax.experimental import pallas as pl
from jax.experimental.pallas import tpu as pltpu

f32 = jnp.float32
bf16 = jnp.bfloat16

D_MODEL = 1024
RW_HEADS = 8
RW_N = 64
RW_W = RW_HEADS * RW_N
RW_DECAY_RANK = 64
RW_ICLR_RANK = 64
RW_GATE_RANK = 128
RW_COLS = 3 * RW_W + RW_DECAY_RANK + RW_ICLR_RANK + RW_GATE_RANK
RW_LN_EPS = 64e-5
RW_CHUNK = 64
SSD_HEADS = 8
SSD_P = 64
SSD_W = SSD_HEADS * SSD_P
SSD_GROUPS = 2
SSD_STATE = 128
SSD_CONV = 4
SSD_CHUNK = 128
SSD_XBC = SSD_W + 2 * SSD_GROUPS * SSD_STATE
SSD_NORM_EPS = 1e-5
SSD_GW = SSD_W // SSD_GROUPS
S5_GROUP = 16
S5_GROUPS = 32
S5_W = S5_GROUP * S5_GROUPS
S5_STATE = 64
S5_SN = S5_GROUPS * S5_STATE
N_BRANCHES = 3
FFN_HIDDEN = 2816
NORM_EPS = 1e-6
LANES = 128
DT_PAD = LANES
IN_CAT = RW_COLS + SSD_W + SSD_XBC + S5_W + DT_PAD

VMEM_LIMIT = 56 * 1024 * 1024


def _cparams(sem):
    return pltpu.CompilerParams(dimension_semantics=sem, vmem_limit_bytes=VMEM_LIMIT)


def _const_spec(shape):
    nd = len(shape)
    return pl.BlockSpec(shape, lambda *_: (0,) * nd, pipeline_mode=pl.Buffered(1))


def _mm(a, b):
    return jnp.dot(a.astype(bf16), b.astype(bf16), preferred_element_type=f32)


def _mm_nt(a, b):
    return lax.dot_general(a.astype(bf16), b.astype(bf16), (((1,), (1,)), ((), ())),
                           preferred_element_type=f32)


def _mm_tn(a, b):
    return lax.dot_general(a.astype(bf16), b.astype(bf16), (((0,), (0,)), ((), ())),
                           preferred_element_type=f32)


def _split(x, parts):
    out = []
    for _ in range(parts - 1):
        hi = x.astype(bf16)
        out.append(hi)
        x = x - hi.astype(f32)
    out.append(x.astype(bf16))
    return out


def _mm_exact_rhs(a, b_bf16, parts):
    acc = None
    for piece in _split(a, parts):
        t = jnp.dot(piece, b_bf16, preferred_element_type=f32)
        acc = t if acc is None else acc + t
    return acc


def _mm_exact_lhs(a_bf16, b, parts):
    acc = None
    for piece in _split(b, parts):
        t = jnp.dot(a_bf16, piece, preferred_element_type=f32)
        acc = t if acc is None else acc + t
    return acc


def _mm_tn_exact_rhs(a, b_bf16, parts):
    acc = None
    for piece in _split(a, parts):
        t = lax.dot_general(piece, b_bf16, (((0,), (0,)), ((), ())), preferred_element_type=f32)
        acc = t if acc is None else acc + t
    return acc


def _rms(x, g, eps):
    return x * lax.rsqrt(jnp.mean(x * x, axis=-1, keepdims=True) + eps) * g


def _sigmoid(x):
    return 1.0 / (1.0 + jnp.exp(-x))


def _softplus(x):
    return jnp.maximum(x, 0.0) + jnp.log(1.0 + jnp.exp(-jnp.abs(x)))


def _silu(x):
    return x * _sigmoid(x)


def _inproj_kernel(x_ref, g_ref, w_ref, p_ref, z_ref, xbc_ref, u_ref, dt_ref):
    hb = _rms(x_ref[...], g_ref[...], NORM_EPS).astype(bf16)
    o = 0
    for ref, width in ((p_ref, RW_COLS), (z_ref, SSD_W), (xbc_ref, SSD_XBC), (u_ref, S5_W),
                       (dt_ref, DT_PAD)):
        ref[...] = jnp.dot(hb, w_ref[:, o:o + width], preferred_element_type=f32)
        o += width


def _inproj(x, g, w_cat, tm):
    B, T, D = x.shape
    nt = T // tm
    row = lambda width: pl.BlockSpec((None, tm, width), lambda b, j: (b, j, 0))
    return pl.pallas_call(
        _inproj_kernel,
        grid=(B, nt),
        in_specs=[row(D), _const_spec((1, D)), _const_spec((D, IN_CAT))],
        out_specs=[row(RW_COLS), row(SSD_W), row(SSD_XBC),
                   pl.BlockSpec((tm, S5_W), lambda b, j: (j, b)),
                   row(DT_PAD)],
        out_shape=[jax.ShapeDtypeStruct((B, T, RW_COLS), f32),
                   jax.ShapeDtypeStruct((B, T, SSD_W), f32),
                   jax.ShapeDtypeStruct((B, T, SSD_XBC), f32),
                   jax.ShapeDtypeStruct((T, B * S5_W), f32),
                   jax.ShapeDtypeStruct((B, T, DT_PAD), f32)],
        compiler_params=_cparams(("parallel", "parallel")),
        name="inproj",
    )(x, g, w_cat)


def _stack_heads(x):
    lane = lax.broadcasted_iota(jnp.int32, x.shape, 1)
    lo = lane < RW_N
    return jnp.concatenate([jnp.where(lo, x, 0.0), jnp.where(lo, 0.0, x)], axis=0)


def _rwkv_kernel(p_ref, mu_ref, w0a0_ref, wwa_ref, gup_ref, kk_ref, ka_ref, rk_ref, lng_ref, lnb_ref,
                 ones_ref, tril_ref, o_ref, prev_ref, h_ref):
    C = RW_CHUNK
    PW = 2 * RW_N
    NP = RW_HEADS // 2

    @pl.when(pl.program_id(1) == 0)
    def _():
        prev_ref[...] = jnp.zeros_like(prev_ref)
        h_ref[...] = jnp.zeros_like(h_ref)

    p = p_ref[...]
    row = lax.broadcasted_iota(jnp.int32, p.shape, 0)
    shifted = jnp.where(row == 0, prev_ref[...], pltpu.roll(p, 1, 0))
    prev_ref[...] = p[C - 1:C, :]
    pm = p + (shifted - p) * mu_ref[...]
    r = pm[:, 0:RW_W]
    k = pm[:, RW_W:2 * RW_W]
    v = pm[:, 2 * RW_W:3 * RW_W]
    wa_in = pm[:, 3 * RW_W:3 * RW_W + PW]
    gd = pm[:, 3 * RW_W + PW:RW_COLS]
    lane = lax.broadcasted_iota(jnp.int32, wa_in.shape, 1)
    wa_in = jnp.where(lane < RW_DECAY_RANK, jnp.tanh(wa_in), wa_in)
    wa = w0a0_ref[...] + _mm(wa_in, wwa_ref[...])
    w_log = -_softplus(-wa[:, :RW_W]) - 0.5
    lw = -jnp.exp(w_log)
    a = _sigmoid(wa[:, RW_W:])
    g = _mm(_sigmoid(gd), gup_ref[...])

    ones = ones_ref[...]
    kk = k * kk_ref[...]
    kk = kk * lax.rsqrt(jnp.maximum(_mm_exact_rhs(kk * kk, ones, 2), 1e-24))
    k2 = k * (1.0 + (a - 1.0) * ka_ref[...])

    lc = _mm_exact_lhs(tril_ref[...], lw, 2)
    lc_last = lc[C - 1:C, :]
    e_pos = jnp.exp(lc)
    e_neg = jnp.exp(-lc)
    e_end = jnp.exp(lc_last - lc)
    kka = kk * a
    r_hat = r * e_pos
    a_hat = -kk * jnp.exp(lc - lw)
    b_hat = kka * e_neg
    k_hat = k2 * e_neg
    b_til = kka * e_end
    k_til = k2 * e_end
    gam = jnp.exp(lc_last)

    ri = lax.broadcasted_iota(jnp.int32, (4 * C, 4 * C), 0)
    ci = lax.broadcasted_iota(jnp.int32, (4 * C, 4 * C), 1)
    tri = (ri % C) + jnp.where(ri < 2 * C, 0, 1) > (ci % C)
    ei = lax.broadcasted_iota(jnp.int32, (PW, PW), 0)
    ej = lax.broadcasted_iota(jnp.int32, (PW, PW), 1)
    eye = ei == ej
    blk = (ei // RW_N) == (ej // RW_N)

    outs = []
    for q in range(NP):
        sl = slice(q * PW, (q + 1) * PW)
        am, rm, bm, km = (_stack_heads(t[:, sl]) for t in (a_hat, r_hat, b_hat, k_hat))
        btm, ktm, vm = (_stack_heads(t[:, sl]) for t in (b_til, k_til, v))
        a_all = _mm_nt(jnp.concatenate([am, rm], axis=0), jnp.concatenate([bm, km], axis=0))
        a_all = jnp.where(tri, a_all, 0.0)
        n_bd = a_all[:2 * C, :2 * C]
        a_ak = a_all[:2 * C, 2 * C:]
        a_rb = a_all[2 * C:, :2 * C]
        a_rk = a_all[2 * C:, 2 * C:]
        t_inv = jnp.where(eye, 1.0, 0.0) + n_bd
        pw = n_bd
        for _ in range(5):
            pw = _mm(pw, pw)
            t_inv = t_inv + _mm(t_inv, pw)
        w1 = _mm(a_ak, vm)
        pq = _mm(t_inv, jnp.concatenate([am, w1], axis=1))
        xy = _mm(a_rb, pq)
        xm = rm + xy[:, :PW]
        ym = xy[:, PW:] + _mm(a_rk, vm)
        mk = _mm_tn(btm, pq[:, :PW])
        gk = _mm_tn(jnp.concatenate([btm, ktm], axis=0), jnp.concatenate([pq[:, PW:], vm], axis=0))
        h = h_ref[q]
        gam_col = jnp.sum(jnp.where(eye, gam[:, sl], 0.0), axis=1, keepdims=True)
        seq = _mm(jnp.concatenate([mk, xm], axis=0), h)
        h_ref[q] = jnp.where(blk, gam_col * h + seq[:PW] + gk, 0.0)
        om = seq[PW:] + ym
        outs.append(om[:C] + om[C:])
    o = jnp.concatenate(outs, axis=1)

    inv_n = 1.0 / RW_N
    mean = _mm_exact_rhs(o, ones, 2) * inv_n
    oc = o - mean
    var = _mm_exact_rhs(oc * oc, ones, 2) * inv_n
    o = oc * lax.rsqrt(var + RW_LN_EPS) * lng_ref[...] + lnb_ref[...]
    bonus = _mm_exact_rhs(r * k2 * rk_ref[...], ones, 2) * v
    o_ref[...] = (o + bonus) * g


def _rwkv(p, consts):
    B, T, _ = p.shape
    C = RW_CHUNK
    return pl.pallas_call(
        _rwkv_kernel,
        grid=(B, T // C),
        in_specs=[pl.BlockSpec((None, C, RW_COLS), lambda b, j: (b, j, 0))]
                 + [_const_spec(c.shape) for c in consts],
        out_specs=pl.BlockSpec((None, C, RW_W), lambda b, j: (b, j, 0)),
        out_shape=jax.ShapeDtypeStruct((B, T, RW_W), f32),
        scratch_shapes=[pltpu.VMEM((1, RW_COLS), f32),
                        pltpu.VMEM((RW_HEADS // 2, 2 * RW_N, 2 * RW_N), f32)],
        compiler_params=_cparams(("parallel", "arbitrary")),
        name="rwkv7",
    )(p, *consts)


def _ssd_kernel(z_ref, xbc_ref, dt_ref, cw_ref, cb_ref, dtb_ref, alog_ref, dskip_ref, ng_ref,
                exp_ref, tril_ref, o_ref, hist_ref, st_ref):
    L = SSD_CHUNK
    HIST = 8

    @pl.when(pl.program_id(1) == 0)
    def _():
        hist_ref[0:HIST, :] = jnp.zeros((HIST, SSD_XBC), f32)
        st_ref[...] = jnp.zeros_like(st_ref)

    hist_ref[HIST:HIST + L, :] = xbc_ref[...]
    conv = cb_ref[...]
    for tap in range(SSD_CONV):
        o0 = HIST - (SSD_CONV - 1) + tap
        conv = conv + cw_ref[tap:tap + 1, :] * hist_ref[o0:o0 + L, :]
    hist_ref[0:HIST, :] = hist_ref[L:L + HIST, :]
    xbc = _silu(conv)
    xs = xbc[:, :SSD_W]
    bmat = xbc[:, SSD_W:SSD_W + SSD_GROUPS * SSD_STATE]
    cmat = xbc[:, SSD_W + SSD_GROUPS * SSD_STATE:]

    dt = _softplus(dt_ref[...] + dtb_ref[...])
    dt_rep = _mm_exact_rhs(dt, exp_ref[...], 3)
    a_rep = -jnp.exp(alog_ref[...])
    da_rep = dt_rep * a_rep
    acs_rep = _mm_exact_lhs(tril_ref[...], da_rep, 3)

    lane = lax.broadcasted_iota(jnp.int32, (L, LANES), 1)
    lo = lane < SSD_P

    def pair64(rep, q):
        return jnp.where(lo, rep[:, (2 * q) * LANES:(2 * q + 1) * LANES],
                         rep[:, (2 * q + 1) * LANES:(2 * q + 2) * LANES])

    dt64 = jnp.concatenate([pair64(dt_rep, q) for q in range(SSD_HEADS // 2)], axis=1)
    acs64 = jnp.concatenate([pair64(acs_rep, q) for q in range(SSD_HEADS // 2)], axis=1)
    xdt = xs * dt64
    acs_last = acs64[L - 1:L, :]
    xdec = xdt * jnp.exp(acs_last - acs64)
    e_acs = jnp.exp(acs64)
    chunk_decay = jnp.exp(acs_last)

    li = lax.broadcasted_iota(jnp.int32, (L, L), 0)
    si = lax.broadcasted_iota(jnp.int32, (L, L), 1)
    causal = li >= si
    eye = li == si
    ys = []
    for gi in range(SSD_GROUPS):
        bg = bmat[:, gi * SSD_STATE:(gi + 1) * SSD_STATE]
        cg = cmat[:, gi * SSD_STATE:(gi + 1) * SSD_STATE]
        scores = _mm_nt(cg, bg)
        hpg = SSD_HEADS // SSD_GROUPS
        wmats, xstack = [], []
        gsl = slice(gi * SSD_GW, (gi + 1) * SSD_GW)
        xg = xdt[:, gsl]
        glane = lax.broadcasted_iota(jnp.int32, xg.shape, 1)
        for j in range(hpg):
            hd = gi * hpg + j
            col = acs_rep[:, hd * LANES:(hd + 1) * LANES]
            rowv = jnp.sum(jnp.where(eye, col, 0.0), axis=0, keepdims=True)
            wmats.append(jnp.where(causal, scores * jnp.exp(col - rowv), 0.0))
            xstack.append(jnp.where((glane // SSD_P) == j, xg, 0.0))
        y_diag = _mm(jnp.concatenate(wmats, axis=1), jnp.concatenate(xstack, axis=0))
        st = st_ref[gi]
        y_off = _mm(cg, st) * e_acs[:, gsl]
        st_ref[gi] = chunk_decay[:, gsl] * st + _mm_tn(bg, xdec[:, gsl])
        ys.append(y_diag + y_off)
    y = jnp.concatenate(ys, axis=1) + xs * dskip_ref[...]
    y = y * _silu(z_ref[...])
    outs = []
    for gi in range(SSD_GROUPS):
        yg = y[:, gi * SSD_GW:(gi + 1) * SSD_GW]
        outs.append(yg * lax.rsqrt(jnp.mean(yg * yg, axis=-1, keepdims=True) + SSD_NORM_EPS))
    o_ref[...] = jnp.concatenate(outs, axis=1) * ng_ref[...]


def _ssd(z, xbc, dt, consts):
    B, T, _ = z.shape
    L = SSD_CHUNK
    row = lambda width: pl.BlockSpec((None, L, width), lambda b, j: (b, j, 0))
    return pl.pallas_call(
        _ssd_kernel,
        grid=(B, T // L),
        in_specs=[row(SSD_W), row(SSD_XBC), row(DT_PAD)] + [_const_spec(c.shape) for c in consts],
        out_specs=row(SSD_W),
        out_shape=jax.ShapeDtypeStruct((B, T, SSD_W), f32),
        scratch_shapes=[pltpu.VMEM((L + 8, SSD_XBC), f32),
                        pltpu.VMEM((SSD_GROUPS, SSD_STATE, SSD_GW), f32)],
        compiler_params=_cparams(("parallel", "arbitrary")),
        name="ssd",
    )(z, xbc, dt, *consts)


def _s5_kernel(u_ref, lre_ref, lim_ref, lstep_ref, bre_ref, bim_ref, cre_ref, cim_ref, d_ref,
               gw_ref, gb_ref, o_ref, sre_ref, sim_ref, bur_ref, bui_ref, *, nb, tt):
    @pl.when(pl.program_id(0) == 0)
    def _():
        sre_ref[...] = jnp.zeros_like(sre_ref)
        sim_ref[...] = jnp.zeros_like(sim_ref)

    lr = lre_ref[...]
    li = lim_ref[...]
    step = jnp.exp(lstep_ref[...])
    mag = jnp.exp(lr * step)
    ab_re = mag * jnp.cos(li * step)
    ab_im = mag * jnp.sin(li * step)
    den = lr * lr + li * li
    coef_re = ((ab_re - 1.0) * lr + ab_im * li) / den
    coef_im = (ab_im * lr - (ab_re - 1.0) * li) / den

    u = u_ref[...]
    ub = u.astype(bf16)
    xr = jnp.dot(ub, bre_ref[...], preferred_element_type=f32)
    xi = jnp.dot(ub, bim_ref[...], preferred_element_type=f32)
    bur_ref[...] = coef_re * xr - coef_im * xi
    bui_ref[...] = coef_re * xi + coef_im * xr

    LB = 2 * LANES
    for jb in range(S5_SN // LB):
        ls = slice(jb * LB, (jb + 1) * LB)
        ar = jnp.broadcast_to(ab_re[:, ls], (nb, LB))
        ai = jnp.broadcast_to(ab_im[:, ls], (nb, LB))

        def body(t, carry):
            s_re, s_im = carry
            rows = pl.ds(pl.multiple_of(t * nb, nb), nb)
            n_re = ar * s_re - ai * s_im + bur_ref[rows, ls]
            n_im = ar * s_im + ai * s_re + bui_ref[rows, ls]
            bur_ref[rows, ls] = n_re
            bui_ref[rows, ls] = n_im
            return n_re, n_im

        s_re, s_im = lax.fori_loop(0, tt, body, (sre_ref[:, ls], sim_ref[:, ls]))
        sre_ref[:, ls] = s_re
        sim_ref[:, ls] = s_im

    y = (jnp.dot(bur_ref[...].astype(bf16), cre_ref[...], preferred_element_type=f32)
         - jnp.dot(bui_ref[...].astype(bf16), cim_ref[...], preferred_element_type=f32))
    y = y + d_ref[...] * u
    zg = 0.5 * y * (1.0 + jnp.tanh(0.7978845608028654 * (y + 0.044715 * (y * y * y))))
    o_ref[...] = zg * _sigmoid(_mm(zg, gw_ref[...]) + gb_ref[...])


def _s5(u_tm, nb, consts, tt):
    rows = u_tm.shape[0]
    blk = tt * nb
    kern = functools.partial(_s5_kernel, nb=nb, tt=tt)
    return pl.pallas_call(
        kern,
        grid=(rows // blk,),
        in_specs=[pl.BlockSpec((blk, S5_W), lambda i: (i, 0))] + [_const_spec(c.shape) for c in consts],
        out_specs=pl.BlockSpec((blk, S5_W), lambda i: (i, 0)),
        out_shape=jax.ShapeDtypeStruct((rows, S5_W), f32),
        scratch_shapes=[pltpu.VMEM((nb, S5_SN), f32), pltpu.VMEM((nb, S5_SN), f32),
                        pltpu.VMEM((blk, S5_SN), f32), pltpu.VMEM((blk, S5_SN), f32)],
        compiler_params=_cparams(("arbitrary",)),
        name="s5",
    )(u_tm, *consts)


def _merge_kernel(x_ref, oa_ref, ob_ref, oc_ref, g_ref, wg_ref, pa_ref, pb_ref, pc_ref, wo_ref, out_ref):
    x = x_ref[...]
    hb = _rms(x, g_ref[...], NORM_EPS).astype(bf16)
    merged = None
    for i, (o_ref, p_ref) in enumerate(((oa_ref, pa_ref), (ob_ref, pb_ref), (oc_ref, pc_ref))):
        gate = _sigmoid(jnp.dot(hb, wg_ref[:, i * D_MODEL:(i + 1) * D_MODEL], preferred_element_type=f32))
        term = gate * jnp.dot(o_ref[...].astype(bf16), p_ref[...], preferred_element_type=f32)
        merged = term if merged is None else merged + term
    out_ref[...] = x + jnp.dot(merged.astype(bf16), wo_ref[...], preferred_element_type=f32)


def _merge(x, oa, ob, oc_tm, g, wg, pa, pb, pc, wo, tm):
    B, T, D = x.shape
    row = lambda width: pl.BlockSpec((None, tm, width), lambda b, j: (b, j, 0))
    return pl.pallas_call(
        _merge_kernel,
        grid=(B, T // tm),
        in_specs=[row(D), row(RW_W), row(SSD_W), pl.BlockSpec((tm, S5_W), lambda b, j: (j, b)),
                  _const_spec((1, D)), _const_spec(wg.shape), _const_spec(pa.shape),
                  _const_spec(pb.shape), _const_spec(pc.shape), _const_spec(wo.shape)],
        out_specs=row(D),
        out_shape=jax.ShapeDtypeStruct((B, T, D), f32),
        compiler_params=_cparams(("parallel", "parallel")),
        name="merge",
    )(x, oa, ob, oc_tm, g, wg, pa, pb, pc, wo)


def _ffn_kernel(x_ref, g_ref, w1_ref, w2_ref, fg_ref, out_ref, *, final):
    x = x_ref[...]
    hb = _rms(x, g_ref[...], NORM_EPS).astype(bf16)
    gate = jnp.dot(hb, w1_ref[:, :FFN_HIDDEN], preferred_element_type=f32)
    up = jnp.dot(hb, w1_ref[:, FFN_HIDDEN:], preferred_element_type=f32)
    act = (_silu(gate) * up).astype(bf16)
    y = x + jnp.dot(act, w2_ref[...], preferred_element_type=f32)
    if final:
        y = _rms(y, fg_ref[...], NORM_EPS)
    out_ref[...] = y


def _ffn(x, g, w1, w2, fg, tm, final):
    B, T, D = x.shape
    row = pl.BlockSpec((None, tm, D), lambda b, j: (b, j, 0))
    return pl.pallas_call(
        functools.partial(_ffn_kernel, final=final),
        grid=(B, T // tm),
        in_specs=[row, _const_spec((1, D)), _const_spec(w1.shape), _const_spec(w2.shape),
                  _const_spec((1, D))],
        out_specs=row,
        out_shape=jax.ShapeDtypeStruct((B, T, D), f32),
        compiler_params=_cparams(("parallel", "parallel")),
        name="ffn",
    )(x, g, w1, w2, fg)


def _row(v):
    return v.reshape(1, -1).astype(f32)


def _block_diag2(a, b):
    za = jnp.zeros((a.shape[0], b.shape[1]), a.dtype)
    zb = jnp.zeros((b.shape[0], a.shape[1]), a.dtype)
    return jnp.concatenate([jnp.concatenate([a, za], axis=1), jnp.concatenate([zb, b], axis=1)], axis=0)


def _tile_tm(T):
    for tm in (512, 256, 128):
        if T % tm == 0:
            return tm
    raise ValueError("sequence length must be a multiple of 128")


def kernel(x, norm_mix, w_in, rwkv_mu, rwkv_w0, rwkv_w_up, rwkv_a0, rwkv_a_up, rwkv_g_up, rwkv_k_k, rwkv_k_a, rwkv_r_k, rwkv_ln_g, rwkv_ln_b, proj_a, ssd_conv_w, ssd_conv_b, ssd_dt_bias, ssd_a_log, ssd_d, ssd_norm_g, proj_b, s5_lam_re, s5_lam_im, s5_log_step, s5_b_re, s5_b_im, s5_c_re, s5_c_im, s5_d, s5_glu_w, s5_glu_b, proj_c, w_out, norm_ffn, ffn_w_in, ffn_w_out, final_norm):
    B, T, D = x.shape
    depth = w_in.shape[0]
    assert D == D_MODEL and T % SSD_CHUNK == 0 and B % 8 == 0
    tm = _tile_tm(T)
    o1 = RW_COLS
    o2 = o1 + SSD_W
    o3 = o2 + SSD_XBC
    o4 = o3 + SSD_HEADS
    o5 = o4 + S5_W

    hid = jnp.arange(RW_W) // RW_N
    head_ones = (hid[:, None] == hid[None, :]).astype(bf16)
    c_idx = jnp.arange(RW_CHUNK)
    tril_rw = (c_idx[:, None] >= c_idx[None, :]).astype(bf16)
    l_idx = jnp.arange(SSD_CHUNK)
    tril_ssd = (l_idx[:, None] >= l_idx[None, :]).astype(bf16)
    expand = (jnp.arange(DT_PAD)[:, None] == (jnp.arange(SSD_HEADS * LANES) // LANES)[None, :]).astype(bf16)
    eye_g = jnp.eye(S5_GROUPS, dtype=f32)
    s5_tt = 32 if T % 32 == 0 else 8

    for l in range(depth):
        wl = w_in[l]
        w_cat = jnp.concatenate(
            [wl[:, :o3], wl[:, o4:o5], wl[:, o3:o4], jnp.zeros((D, DT_PAD - SSD_HEADS), f32)],
            axis=1).astype(bf16)
        p, z, xbc, u_tm, dt = _inproj(x, _row(norm_mix[l]), w_cat, tm)

        rw_consts = (
            _row(rwkv_mu[l]),
            jnp.concatenate([_row(rwkv_w0[l]), _row(rwkv_a0[l])], axis=1),
            _block_diag2(rwkv_w_up[l], rwkv_a_up[l]).astype(bf16),
            rwkv_g_up[l].astype(bf16),
            _row(rwkv_k_k[l]), _row(rwkv_k_a[l]), _row(rwkv_r_k[l]),
            _row(rwkv_ln_g[l]), _row(rwkv_ln_b[l]), head_ones, tril_rw)
        oa = _rwkv(p, rw_consts)

        ssd_consts = (
            jnp.concatenate([ssd_conv_w[l], jnp.zeros((8 - SSD_CONV, SSD_XBC), f32)], axis=0),
            _row(ssd_conv_b[l]),
            jnp.concatenate([_row(ssd_dt_bias[l]), jnp.zeros((1, DT_PAD - SSD_HEADS), f32)], axis=1),
            _row(jnp.repeat(ssd_a_log[l], LANES)),
            _row(jnp.repeat(ssd_d[l], SSD_P)),
            _row(ssd_norm_g[l]), expand, tril_ssd)
        ob = _ssd(z, xbc, dt, ssd_consts)

        s5_consts = (
            _row(s5_lam_re[l]), _row(s5_lam_im[l]), _row(jnp.repeat(s5_log_step[l], S5_STATE)),
            jnp.einsum('gni,gh->gihn', s5_b_re[l], eye_g).reshape(S5_W, S5_SN).astype(bf16),
            jnp.einsum('gni,gh->gihn', s5_b_im[l], eye_g).reshape(S5_W, S5_SN).astype(bf16),
            jnp.einsum('gon,gh->gnho', s5_c_re[l], eye_g).reshape(S5_SN, S5_W).astype(bf16),
            jnp.einsum('gon,gh->gnho', s5_c_im[l], eye_g).reshape(S5_SN, S5_W).astype(bf16),
            _row(s5_d[l]), s5_glu_w[l].astype(bf16), _row(s5_glu_b[l]))
        oc_tm = _s5(u_tm.reshape(T * B, S5_W), B, s5_consts, s5_tt).reshape(T, B * S5_W)

        x = _merge(x, oa, ob, oc_tm, _row(norm_mix[l]), wl[:, o5:].astype(bf16),
                   proj_a[l].astype(bf16), proj_b[l].astype(bf16), proj_c[l].astype(bf16),
                   w_out[l].astype(bf16), tm)
        x = _ffn(x, _row(norm_ffn[l]), ffn_w_in[l].astype(bf16), ffn_w_out[l].astype(bf16),
                 _row(final_norm), tm, final=(l == depth - 1))
    return x
```

```python
import functools

import jax
import jax.numpy as jnp
from jax import lax
from jax.experimental import pallas as pl
from jax.experimental.pallas import tpu as pltpu

f32 = jnp.float32
bf16 = jnp.bfloat16

D_MODEL = 1024
RW_HEADS = 8
RW_N = 64
RW_W = RW_HEADS * RW_N
RW_DECAY_RANK = 64
RW_ICLR_RANK = 64
RW_GATE_RANK = 128
RW_COLS = 3 * RW_W + RW_DECAY_RANK + RW_ICLR_RANK + RW_GATE_RANK
RW_LN_EPS = 64e-5
RW_CHUNK = 64
RW_CHUNKS_PER_STEP = 4
SSD_HEADS = 8
SSD_P = 64
SSD_W = SSD_HEADS * SSD_P
SSD_GROUPS = 2
SSD_STATE = 128
SSD_CONV = 4
SSD_CHUNK = 128
SSD_XBC = SSD_W + 2 * SSD_GROUPS * SSD_STATE
SSD_NORM_EPS = 1e-5
SSD_GW = SSD_W // SSD_GROUPS
S5_GROUP = 16
S5_GROUPS = 32
S5_W = S5_GROUP * S5_GROUPS
S5_STATE = 64
S5_SN = S5_GROUPS * S5_STATE
N_BRANCHES = 3
FFN_HIDDEN = 2816
NORM_EPS = 1e-6
LANES = 128
DT_PAD = LANES
IN_CAT = RW_COLS + SSD_W + SSD_XBC + S5_W + DT_PAD

VMEM_LIMIT = 56 * 1024 * 1024


def _cparams(sem):
    return pltpu.CompilerParams(dimension_semantics=sem, vmem_limit_bytes=VMEM_LIMIT)


def _const_spec(arr):
    nd = arr.ndim
    return pl.BlockSpec(arr.shape, lambda *_: (0,) * nd, pipeline_mode=pl.Buffered(1))


def _layer_spec(arr, l):
    nd = arr.ndim - 1
    return pl.BlockSpec((None,) + arr.shape[1:], lambda *_: (l,) + (0,) * nd,
                        pipeline_mode=pl.Buffered(1))


def _mm(a, b):
    return jnp.dot(a.astype(bf16), b.astype(bf16), preferred_element_type=f32)


def _mm_nt(a, b):
    return lax.dot_general(a.astype(bf16), b.astype(bf16), (((1,), (1,)), ((), ())),
                           preferred_element_type=f32)


def _mm_tn(a, b):
    return lax.dot_general(a.astype(bf16), b.astype(bf16), (((0,), (0,)), ((), ())),
                           preferred_element_type=f32)


def _split(x, parts):
    out = []
    for _ in range(parts - 1):
        hi = x.astype(bf16)
        out.append(hi)
        x = x - hi.astype(f32)
    out.append(x.astype(bf16))
    return out


def _mm_exact_rhs(a, b_bf16, parts):
    acc = None
    for piece in _split(a, parts):
        t = jnp.dot(piece, b_bf16, preferred_element_type=f32)
        acc = t if acc is None else acc + t
    return acc


def _mm_exact_lhs(a_bf16, b, parts):
    acc = None
    for piece in _split(b, parts):
        t = jnp.dot(a_bf16, piece, preferred_element_type=f32)
        acc = t if acc is None else acc + t
    return acc


def _rms(x, g, eps):
    return x * lax.rsqrt(jnp.mean(x * x, axis=-1, keepdims=True) + eps) * g


def _sigmoid(x):
    return 1.0 / (1.0 + jnp.exp(-x))


def _softplus(x):
    return jnp.maximum(x, 0.0) + jnp.log(1.0 + jnp.exp(-jnp.abs(x)))


def _silu(x):
    return x * _sigmoid(x)


def _inproj_kernel(x_ref, g_ref, w_ref, p_ref, z_ref, xbc_ref, u_ref, dt_ref):
    hb = _rms(x_ref[...], g_ref[...], NORM_EPS).astype(bf16)
    o = 0
    for ref, width in ((p_ref, RW_COLS), (z_ref, SSD_W), (xbc_ref, SSD_XBC), (u_ref, S5_W),
                       (dt_ref, DT_PAD)):
        ref[...] = jnp.dot(hb, w_ref[:, o:o + width], preferred_element_type=f32)
        o += width


def _inproj(x, g, w_cat, l, tm):
    B, T, D = x.shape
    nt = T // tm
    row = lambda width: pl.BlockSpec((None, tm, width), lambda b, j: (b, j, 0))
    return pl.pallas_call(
        _inproj_kernel,
        grid=(B, nt),
        in_specs=[row(D), _layer_spec(g, l), _layer_spec(w_cat, l)],
        out_specs=[row(RW_COLS), row(SSD_W), row(SSD_XBC),
                   pl.BlockSpec((tm, S5_W), lambda b, j: (j, b)),
                   row(DT_PAD)],
        out_shape=[jax.ShapeDtypeStruct((B, T, RW_COLS), f32),
                   jax.ShapeDtypeStruct((B, T, SSD_W), f32),
                   jax.ShapeDtypeStruct((B, T, SSD_XBC), f32),
                   jax.ShapeDtypeStruct((T, B * S5_W), f32),
                   jax.ShapeDtypeStruct((B, T, DT_PAD), f32)],
        compiler_params=_cparams(("parallel", "parallel")),
        name="inproj",
    )(x, g, w_cat)


def _stack_heads(x):
    lane = lax.broadcasted_iota(jnp.int32, x.shape, 1)
    lo = lane < RW_N
    return jnp.concatenate([jnp.where(lo, x, 0.0), jnp.where(lo, 0.0, x)], axis=0)


def _head_sums(x, ones_pair):
    R = x.shape[0]
    PW = 2 * RW_N
    st = jnp.concatenate([x[:, q * PW:(q + 1) * PW] for q in range(RW_W // PW)], axis=0)
    s = _mm_exact_rhs(st, ones_pair, 2)
    return jnp.concatenate([s[q * R:(q + 1) * R] for q in range(RW_W // PW)], axis=1)


def _rwkv_kernel(p_ref, mu_ref, w0a0_ref, wwa_ref, gup_ref, kk_ref, ka_ref, rk_ref, lng_ref, lnb_ref,
                 ones_ref, tril_ref, o_ref, prev_ref, h_ref, *, nc):
    C = RW_CHUNK
    TB = nc * C
    PW = 2 * RW_N
    NP = RW_HEADS // 2

    @pl.when(pl.program_id(1) == 0)
    def _():
        prev_ref[...] = jnp.zeros_like(prev_ref)
        h_ref[...] = jnp.zeros_like(h_ref)

    p = p_ref[...]
    row = lax.broadcasted_iota(jnp.int32, p.shape, 0)
    shifted = jnp.where(row == 0, prev_ref[...], pltpu.roll(p, 1, 0))
    prev_ref[...] = p[TB - 1:TB, :]
    pm = p + (shifted - p) * mu_ref[...]
    r = pm[:, 0:RW_W]
    k = pm[:, RW_W:2 * RW_W]
    v = pm[:, 2 * RW_W:3 * RW_W]
    wa_in = pm[:, 3 * RW_W:3 * RW_W + PW]
    gd = pm[:, 3 * RW_W + PW:RW_COLS]
    lane = lax.broadcasted_iota(jnp.int32, wa_in.shape, 1)
    wa_in = jnp.where(lane < RW_DECAY_RANK, jnp.tanh(wa_in), wa_in)
    wa = w0a0_ref[...] + _mm(wa_in, wwa_ref[...])
    w_log = -_softplus(-wa[:, :RW_W]) - 0.5
    lw = -jnp.exp(w_log)
    a = _sigmoid(wa[:, RW_W:])
    g = _mm(_sigmoid(gd), gup_ref[...])

    ones = ones_ref[...]
    kk = k * kk_ref[...]
    kk = kk * lax.rsqrt(jnp.maximum(_head_sums(kk * kk, ones), 1e-24))
    k2 = k * (1.0 + (a - 1.0) * ka_ref[...])

    lc = _mm_exact_lhs(tril_ref[...], lw, 2)
    lc_last = jnp.concatenate(
        [jnp.broadcast_to(lc[(c + 1) * C - 1:(c + 1) * C, :], (C, RW_W)) for c in range(nc)], axis=0)
    e_pos = jnp.exp(lc)
    e_neg = jnp.exp(-lc)
    e_end = jnp.exp(lc_last - lc)
    kka = kk * a
    r_hat = r * e_pos
    a_hat = -kk * jnp.exp(lc - lw)
    b_hat = kka * e_neg
    k_hat = k2 * e_neg
    b_til = kka * e_end
    k_til = k2 * e_end

    ri = lax.broadcasted_iota(jnp.int32, (4 * C, 4 * C), 0)
    ci = lax.broadcasted_iota(jnp.int32, (4 * C, 4 * C), 1)
    tri = (ri % C) + jnp.where(ri < 2 * C, 0, 1) > (ci % C)
    ei = lax.broadcasted_iota(jnp.int32, (PW, PW), 0)
    ej = lax.broadcasted_iota(jnp.int32, (PW, PW), 1)
    eye = ei == ej
    blk = (ei // RW_N) == (ej // RW_N)
    eye_f = jnp.where(eye, 1.0, 0.0)

    inst = [(c, q) for c in range(nc) for q in range(NP)]
    every = range(len(inst))

    def stacked(t):
        return [_stack_heads(t[c * C:(c + 1) * C, q * PW:(q + 1) * PW]) for c, q in inst]

    am, rm, bm, km = stacked(a_hat), stacked(r_hat), stacked(b_hat), stacked(k_hat)
    btm, ktm, vm = stacked(b_til), stacked(k_til), stacked(v)
    a_all = [_mm_nt(jnp.concatenate([am[i], rm[i]], axis=0), jnp.concatenate([bm[i], km[i]], axis=0))
             for i in every]
    a_all = [jnp.where(tri, t, 0.0) for t in a_all]
    pw = [t[:2 * C, :2 * C] for t in a_all]
    t_inv = [eye_f + t for t in pw]
    for _ in range(5):
        pw = [_mm(t, t) for t in pw]
        t_inv = [t_inv[i] + _mm(t_inv[i], pw[i]) for i in every]
    av = [_mm(a_all[i][:, 2 * C:], vm[i]) for i in every]
    pq = [_mm(t_inv[i], jnp.concatenate([am[i], av[i][:2 * C]], axis=1)) for i in every]
    xy = [_mm(a_all[i][2 * C:, :2 * C], pq[i]) for i in every]
    mk = [_mm_tn(btm[i], pq[i][:, :PW]) for i in every]
    gk = [_mm_tn(jnp.concatenate([btm[i], ktm[i]], axis=0), jnp.concatenate([pq[i][:, PW:], vm[i]], axis=0))
          for i in every]
    mx = [jnp.concatenate([mk[i], rm[i] + xy[i][:, :PW]], axis=0) for i in every]
    ym = [xy[i][:, PW:] + av[i][2 * C:] for i in every]
    gam_col = [jnp.sum(jnp.where(eye, jnp.exp(lc[(c + 1) * C - 1:(c + 1) * C, q * PW:(q + 1) * PW]), 0.0),
                       axis=1, keepdims=True) for c, q in inst]

    hs = [h_ref[q] for q in range(NP)]
    o_rows = []
    for c in range(nc):
        seq = [_mm(mx[c * NP + q], hs[q]) for q in range(NP)]
        hs = [jnp.where(blk, gam_col[c * NP + q] * hs[q] + seq[q][:PW] + gk[c * NP + q], 0.0) for q in range(NP)]
        om = [seq[q][PW:] + ym[c * NP + q] for q in range(NP)]
        o_rows.append(jnp.concatenate([t[:C] + t[C:] for t in om], axis=1))
    for q in range(NP):
        h_ref[q] = hs[q]
    o = jnp.concatenate(o_rows, axis=0)

    inv_n = 1.0 / RW_N
    mean = _head_sums(o, ones) * inv_n
    oc = o - mean
    var = _head_sums(oc * oc, ones) * inv_n
    o = oc * lax.rsqrt(var + RW_LN_EPS) * lng_ref[...] + lnb_ref[...]
    bonus = _head_sums(r * k2 * rk_ref[...], ones) * v
    o_ref[...] = (o + bonus) * g


def _rwkv(p, layer_consts, shared_consts, l, nc):
    B, T, _ = p.shape
    TB = nc * RW_CHUNK
    return pl.pallas_call(
        functools.partial(_rwkv_kernel, nc=nc),
        grid=(B, T // TB),
        in_specs=[pl.BlockSpec((None, TB, RW_COLS), lambda b, j: (b, j, 0))]
                 + [_layer_spec(c, l) for c in layer_consts] + [_const_spec(c) for c in shared_consts],
        out_specs=pl.BlockSpec((None, TB, RW_W), lambda b, j: (b, j, 0)),
        out_shape=jax.ShapeDtypeStruct((B, T, RW_W), f32),
        scratch_shapes=[pltpu.VMEM((1, RW_COLS), f32),
                        pltpu.VMEM((RW_HEADS // 2, 2 * RW_N, 2 * RW_N), f32)],
        compiler_params=_cparams(("parallel", "arbitrary")),
        name="rwkv7",
    )(p, *layer_consts, *shared_consts)


def _ssd_kernel(z_ref, xbc_ref, dt_ref, cw_ref, cb_ref, dtb_ref, alog_ref, dskip_ref, ng_ref,
                exp_ref, tril_ref, o_ref, hist_ref, st_ref):
    L = SSD_CHUNK
    HIST = 8

    @pl.when(pl.program_id(1) == 0)
    def _():
        hist_ref[0:HIST, :] = jnp.zeros((HIST, SSD_XBC), f32)
        st_ref[...] = jnp.zeros_like(st_ref)

    hist_ref[HIST:HIST + L, :] = xbc_ref[...]
    conv = cb_ref[...]
    for tap in range(SSD_CONV):
        o0 = HIST - (SSD_CONV - 1) + tap
        conv = conv + cw_ref[tap:tap + 1, :] * hist_ref[o0:o0 + L, :]
    hist_ref[0:HIST, :] = hist_ref[L:L + HIST, :]
    xbc = _silu(conv)
    xs = xbc[:, :SSD_W]
    bmat = xbc[:, SSD_W:SSD_W + SSD_GROUPS * SSD_STATE]
    cmat = xbc[:, SSD_W + SSD_GROUPS * SSD_STATE:]

    dt = _softplus(dt_ref[...] + dtb_ref[...])
    dt_rep = _mm_exact_rhs(dt, exp_ref[...], 3)
    a_rep = -jnp.exp(alog_ref[...])
    da_rep = dt_rep * a_rep
    acs_rep = _mm_exact_lhs(tril_ref[...], da_rep, 3)

    lane = lax.broadcasted_iota(jnp.int32, (L, LANES), 1)
    lo = lane < SSD_P

    def pair64(rep, q):
        return jnp.where(lo, rep[:, (2 * q) * LANES:(2 * q + 1) * LANES],
                         rep[:, (2 * q + 1) * LANES:(2 * q + 2) * LANES])

    dt64 = jnp.concatenate([pair64(dt_rep, q) for q in range(SSD_HEADS // 2)], axis=1)
    acs64 = jnp.concatenate([pair64(acs_rep, q) for q in range(SSD_HEADS // 2)], axis=1)
    xdt = xs * dt64
    acs_last = acs64[L - 1:L, :]
    xdec = xdt * jnp.exp(acs_last - acs64)
    e_acs = jnp.exp(acs64)
    chunk_decay = jnp.exp(acs_last)

    li = lax.broadcasted_iota(jnp.int32, (L, L), 0)
    si = lax.broadcasted_iota(jnp.int32, (L, L), 1)
    causal = li >= si
    eye = li == si
    ys = []
    for gi in range(SSD_GROUPS):
        bg = bmat[:, gi * SSD_STATE:(gi + 1) * SSD_STATE]
        cg = cmat[:, gi * SSD_STATE:(gi + 1) * SSD_STATE]
        scores = _mm_nt(cg, bg)
        hpg = SSD_HEADS // SSD_GROUPS
        wmats, xstack = [], []
        gsl = slice(gi * SSD_GW, (gi + 1) * SSD_GW)
        xg = xdt[:, gsl]
        glane = lax.broadcasted_iota(jnp.int32, xg.shape, 1)
        for j in range(hpg):
            hd = gi * hpg + j
            col = acs_rep[:, hd * LANES:(hd + 1) * LANES]
            rowv = jnp.sum(jnp.where(eye, col, 0.0), axis=0, keepdims=True)
            wmats.append(jnp.where(causal, scores * jnp.exp(col - rowv), 0.0))
            xstack.append(jnp.where((glane // SSD_P) == j, xg, 0.0))
        y_diag = _mm(jnp.concatenate(wmats, axis=1), jnp.concatenate(xstack, axis=0))
        st = st_ref[gi]
        y_off = _mm(cg, st) * e_acs[:, gsl]
        st_ref[gi] = chunk_decay[:, gsl] * st + _mm_tn(bg, xdec[:, gsl])
        ys.append(y_diag + y_off)
    y = jnp.concatenate(ys, axis=1) + xs * dskip_ref[...]
    y = y * _silu(z_ref[...])
    outs = []
    for gi in range(SSD_GROUPS):
        yg = y[:, gi * SSD_GW:(gi + 1) * SSD_GW]
        outs.append(yg * lax.rsqrt(jnp.mean(yg * yg, axis=-1, keepdims=True) + SSD_NORM_EPS))
    o_ref[...] = jnp.concatenate(outs, axis=1) * ng_ref[...]


def _ssd(z, xbc, dt, layer_consts, shared_consts, l):
    B, T, _ = z.shape
    L = SSD_CHUNK
    row = lambda width: pl.BlockSpec((None, L, width), lambda b, j: (b, j, 0))
    return pl.pallas_call(
        _ssd_kernel,
        grid=(B, T // L),
        in_specs=[row(SSD_W), row(SSD_XBC), row(DT_PAD)]
                 + [_layer_spec(c, l) for c in layer_consts] + [_const_spec(c) for c in shared_consts],
        out_specs=row(SSD_W),
        out_shape=jax.ShapeDtypeStruct((B, T, SSD_W), f32),
        scratch_shapes=[pltpu.VMEM((L + 8, SSD_XBC), f32),
                        pltpu.VMEM((SSD_GROUPS, SSD_STATE, SSD_GW), f32)],
        compiler_params=_cparams(("parallel", "arbitrary")),
        name="ssd",
    )(z, xbc, dt, *layer_consts, *shared_consts)


def _s5_kernel(u_ref, lre_ref, lim_ref, lstep_ref, wb_ref, cre_ref, cim_ref, d_ref,
               gw_ref, gb_ref, o_ref, sre_ref, sim_ref, bur_ref, bui_ref, par_ref, *, nb, tt):
    @pl.when(pl.program_id(0) == 0)
    def _():
        sre_ref[...] = jnp.zeros_like(sre_ref)
        sim_ref[...] = jnp.zeros_like(sim_ref)
        lr = lre_ref[...]
        li = lim_ref[...]
        step = jnp.exp(lstep_ref[...])
        mag = jnp.exp(lr * step)
        a_re = mag * jnp.cos(li * step)
        a_im = mag * jnp.sin(li * step)
        den = lr * lr + li * li
        par_ref[0:1, :] = a_re
        par_ref[1:2, :] = a_im
        par_ref[2:3, :] = ((a_re - 1.0) * lr + a_im * li) / den
        par_ref[3:4, :] = (a_im * lr - (a_re - 1.0) * li) / den

    ab_re = par_ref[0:1, :]
    ab_im = par_ref[1:2, :]
    coef_re = par_ref[2:3, :]
    coef_im = par_ref[3:4, :]

    u = u_ref[...]
    ub = u.astype(bf16)
    in_per_tile = LANES // S5_STATE * S5_GROUP
    for j in range(S5_SN // LANES):
        us = slice(j * in_per_tile // LANES * LANES, (j * in_per_tile // LANES + 1) * LANES)
        ls = slice(j * LANES, (j + 1) * LANES)
        xri = jnp.dot(ub[:, us], wb_ref[us, 2 * j * LANES:2 * (j + 1) * LANES], preferred_element_type=f32)
        xr = xri[:, :LANES]
        xi = xri[:, LANES:]
        bur_ref[:, ls] = coef_re[:, ls] * xr - coef_im[:, ls] * xi
        bui_ref[:, ls] = coef_re[:, ls] * xi + coef_im[:, ls] * xr

    LB = 2 * LANES
    for jb in range(S5_SN // LB):
        ls = slice(jb * LB, (jb + 1) * LB)
        ar = jnp.broadcast_to(ab_re[:, ls], (nb, LB))
        ai = jnp.broadcast_to(ab_im[:, ls], (nb, LB))

        def body(t, carry):
            s_re, s_im = carry
            rows = pl.ds(pl.multiple_of(t * nb, nb), nb)
            n_re = ar * s_re - ai * s_im + bur_ref[rows, ls]
            n_im = ar * s_im + ai * s_re + bui_ref[rows, ls]
            bur_ref[rows, ls] = n_re
            bui_ref[rows, ls] = n_im
            return n_re, n_im

        s_re, s_im = lax.fori_loop(0, tt, body, (sre_ref[:, ls], sim_ref[:, ls]))
        sre_ref[:, ls] = s_re
        sim_ref[:, ls] = s_im

    st_per_tile = LANES // S5_GROUP * S5_STATE
    ys = []
    for m in range(S5_W // LANES):
        ks = slice(m * st_per_tile, (m + 1) * st_per_tile)
        os_ = slice(m * LANES, (m + 1) * LANES)
        ys.append(jnp.dot(bur_ref[:, ks].astype(bf16), cre_ref[ks, os_], preferred_element_type=f32)
                  - jnp.dot(bui_ref[:, ks].astype(bf16), cim_ref[ks, os_], preferred_element_type=f32))
    y = jnp.concatenate(ys, axis=1) + d_ref[...] * u
    zg = 0.5 * y * (1.0 + jnp.tanh(0.7978845608028654 * (y + 0.044715 * (y * y * y))))
    o_ref[...] = zg * _sigmoid(_mm(zg, gw_ref[...]) + gb_ref[...])


def _s5(u_tm, nb, layer_consts, l, tt):
    rows = u_tm.shape[0]
    blk = tt * nb
    kern = functools.partial(_s5_kernel, nb=nb, tt=tt)
    return pl.pallas_call(
        kern,
        grid=(rows // blk,),
        in_specs=[pl.BlockSpec((blk, S5_W), lambda i: (i, 0))] + [_layer_spec(c, l) for c in layer_consts],
        out_specs=pl.BlockSpec((blk, S5_W), lambda i: (i, 0)),
        out_shape=jax.ShapeDtypeStruct((rows, S5_W), f32),
        scratch_shapes=[pltpu.VMEM((nb, S5_SN), f32), pltpu.VMEM((nb, S5_SN), f32),
                        pltpu.VMEM((blk, S5_SN), f32), pltpu.VMEM((blk, S5_SN), f32),
                        pltpu.VMEM((8, S5_SN), f32)],
        compiler_params=_cparams(("arbitrary",)),
        name="s5",
    )(u_tm, *layer_consts)


def _merge_kernel(x_ref, oa_ref, ob_ref, oc_ref, g_ref, wg_ref, pa_ref, pb_ref, pc_ref, wo_ref, out_ref):
    x = x_ref[...]
    hb = _rms(x, g_ref[...], NORM_EPS).astype(bf16)
    merged = None
    for i, (o_ref, p_ref) in enumerate(((oa_ref, pa_ref), (ob_ref, pb_ref), (oc_ref, pc_ref))):
        gate = _sigmoid(jnp.dot(hb, wg_ref[:, i * D_MODEL:(i + 1) * D_MODEL], preferred_element_type=f32))
        term = gate * jnp.dot(o_ref[...].astype(bf16), p_ref[...], preferred_element_type=f32)
        merged = term if merged is None else merged + term
    out_ref[...] = x + jnp.dot(merged.astype(bf16), wo_ref[...], preferred_element_type=f32)


def _merge(x, oa, ob, oc_tm, layer_consts, l, tm):
    B, T, D = x.shape
    row = lambda width: pl.BlockSpec((None, tm, width), lambda b, j: (b, j, 0))
    return pl.pallas_call(
        _merge_kernel,
        grid=(B, T // tm),
        in_specs=[row(D), row(RW_W), row(SSD_W), pl.BlockSpec((tm, S5_W), lambda b, j: (j, b))]
                 + [_layer_spec(c, l) for c in layer_consts],
        out_specs=row(D),
        out_shape=jax.ShapeDtypeStruct((B, T, D), f32),
        compiler_params=_cparams(("parallel", "parallel")),
        name="merge",
    )(x, oa, ob, oc_tm, *layer_consts)


def _ffn_kernel(x_ref, g_ref, w1_ref, w2_ref, fg_ref, out_ref, *, final):
    x = x_ref[...]
    hb = _rms(x, g_ref[...], NORM_EPS).astype(bf16)
    gate = jnp.dot(hb, w1_ref[:, :FFN_HIDDEN], preferred_element_type=f32)
    up = jnp.dot(hb, w1_ref[:, FFN_HIDDEN:], preferred_element_type=f32)
    act = (_silu(gate) * up).astype(bf16)
    y = x + jnp.dot(act, w2_ref[...], preferred_element_type=f32)
    if final:
        y = _rms(y, fg_ref[...], NORM_EPS)
    out_ref[...] = y


def _ffn(x, layer_consts, fg, l, tm, final):
    B, T, D = x.shape
    row = pl.BlockSpec((None, tm, D), lambda b, j: (b, j, 0))
    return pl.pallas_call(
        functools.partial(_ffn_kernel, final=final),
        grid=(B, T // tm),
        in_specs=[row] + [_layer_spec(c, l) for c in layer_consts] + [_const_spec(fg)],
        out_specs=row,
        out_shape=jax.ShapeDtypeStruct((B, T, D), f32),
        compiler_params=_cparams(("parallel", "parallel")),
        name="ffn",
    )(x, *layer_consts, fg)


def _rows(v):
    return v.reshape(v.shape[0], 1, -1).astype(f32)


def _s5_block_diag(w2d, row_group, col_group):
    tiled = jnp.tile(w2d, (1, 1, S5_GROUPS))
    r = jnp.arange(tiled.shape[1]) // row_group
    c = jnp.arange(tiled.shape[2]) // col_group
    return jnp.where((r[:, None] == c[None, :])[None], tiled, 0.0).astype(bf16)


def _tile_tm(T):
    for tm in (512, 256, 128):
        if T % tm == 0:
            return tm
    raise ValueError("sequence length must be a multiple of 128")


def kernel(x, norm_mix, w_in, rwkv_mu, rwkv_w0, rwkv_w_up, rwkv_a0, rwkv_a_up, rwkv_g_up, rwkv_k_k, rwkv_k_a, rwkv_r_k, rwkv_ln_g, rwkv_ln_b, proj_a, ssd_conv_w, ssd_conv_b, ssd_dt_bias, ssd_a_log, ssd_d, ssd_norm_g, proj_b, s5_lam_re, s5_lam_im, s5_log_step, s5_b_re, s5_b_im, s5_c_re, s5_c_im, s5_d, s5_glu_w, s5_glu_b, proj_c, w_out, norm_ffn, ffn_w_in, ffn_w_out, final_norm):
    B, T, D = x.shape
    depth = w_in.shape[0]
    rw_nc = RW_CHUNKS_PER_STEP
    assert D == D_MODEL and T % SSD_CHUNK == 0 and B % 8 == 0 and T % (rw_nc * RW_CHUNK) == 0
    tm = _tile_tm(T)
    o1 = RW_COLS
    o2 = o1 + SSD_W
    o3 = o2 + SSD_XBC
    o4 = o3 + SSD_HEADS
    o5 = o4 + S5_W

    hid = jnp.arange(2 * RW_N) // RW_N
    head_ones = (hid[:, None] == hid[None, :]).astype(bf16)
    c_idx = jnp.arange(rw_nc * RW_CHUNK)
    tril_rw = ((c_idx[:, None] >= c_idx[None, :])
               & (c_idx[:, None] // RW_CHUNK == c_idx[None, :] // RW_CHUNK)).astype(bf16)
    l_idx = jnp.arange(SSD_CHUNK)
    tril_ssd = (l_idx[:, None] >= l_idx[None, :]).astype(bf16)
    expand = (jnp.arange(DT_PAD)[:, None] == (jnp.arange(SSD_HEADS * LANES) // LANES)[None, :]).astype(bf16)

    g_mix = _rows(norm_mix)
    w_cat = jnp.concatenate(
        [w_in[:, :, :o3], w_in[:, :, o4:o5], w_in[:, :, o3:o4],
         jnp.zeros((depth, D, DT_PAD - SSD_HEADS), f32)], axis=2).astype(bf16)
    zr = jnp.zeros((depth, RW_DECAY_RANK, RW_W), f32)
    w_wa = jnp.concatenate([jnp.concatenate([rwkv_w_up, zr], axis=2),
                            jnp.concatenate([zr, rwkv_a_up], axis=2)], axis=1).astype(bf16)
    rw_layer = (_rows(rwkv_mu), jnp.concatenate([_rows(rwkv_w0), _rows(rwkv_a0)], axis=2), w_wa,
                rwkv_g_up.astype(bf16), _rows(rwkv_k_k), _rows(rwkv_k_a), _rows(rwkv_r_k),
                _rows(rwkv_ln_g), _rows(rwkv_ln_b))
    ssd_layer = (
        jnp.concatenate([ssd_conv_w, jnp.zeros((depth, 8 - SSD_CONV, SSD_XBC), f32)], axis=1),
        _rows(ssd_conv_b),
        jnp.concatenate([_rows(ssd_dt_bias), jnp.zeros((depth, 1, DT_PAD - SSD_HEADS), f32)], axis=2),
        _rows(jnp.repeat(ssd_a_log, LANES, axis=1)),
        _rows(jnp.repeat(ssd_d, SSD_P, axis=1)),
        _rows(ssd_norm_g))
    b_rows = lambda b: jnp.swapaxes(b, 2, 3).reshape(depth, S5_W, S5_STATE)
    c_rows = lambda c: jnp.swapaxes(c, 2, 3).reshape(depth, S5_SN, S5_GROUP)
    nlt = S5_SN // LANES
    w_b = jnp.stack([_s5_block_diag(b_rows(s5_b_re), S5_GROUP, S5_STATE).reshape(depth, S5_W, nlt, LANES),
                     _s5_block_diag(b_rows(s5_b_im), S5_GROUP, S5_STATE).reshape(depth, S5_W, nlt, LANES)],
                    axis=3).reshape(depth, S5_W, 2 * S5_SN)
    s5_layer = (
        _rows(s5_lam_re), _rows(s5_lam_im), _rows(jnp.repeat(s5_log_step, S5_STATE, axis=1)),
        w_b,
        _s5_block_diag(c_rows(s5_c_re), S5_STATE, S5_GROUP),
        _s5_block_diag(c_rows(s5_c_im), S5_STATE, S5_GROUP),
        _rows(s5_d), s5_glu_w.astype(bf16), _rows(s5_glu_b))
    merge_layer = (g_mix, w_in[:, :, o5:].astype(bf16), proj_a.astype(bf16), proj_b.astype(bf16),
                   proj_c.astype(bf16), w_out.astype(bf16))
    ffn_layer = (_rows(norm_ffn), ffn_w_in.astype(bf16), ffn_w_out.astype(bf16))
    fg = final_norm.reshape(1, D).astype(f32)
    s5_tt = 32 if T % 32 == 0 else 8

    for l in range(depth):
        p, z, xbc, u_tm, dt = _inproj(x, g_mix, w_cat, l, tm)
        oa = _rwkv(p, rw_layer, (head_ones, tril_rw), l, rw_nc)
        ob = _ssd(z, xbc, dt, ssd_layer, (expand, tril_ssd), l)
        oc_tm = _s5(u_tm.reshape(T * B, S5_W), B, s5_layer, l, s5_tt).reshape(T, B * S5_W)
        x = _merge(x, oa, ob, oc_tm, merge_layer, l, tm)
        x = _ffn(x, ffn_layer, fg, l, tm, final=(l == depth - 1))
    return x
```

```python
import functools

import jax
import jax.numpy as jnp
from jax import lax
from jax.experimental import pallas as pl
from jax.experimental.pallas import tpu as pltpu

f32 = jnp.float32
bf16 = jnp.bfloat16

D_MODEL = 1024
RW_HEADS = 8
RW_N = 64
RW_W = RW_HEADS * RW_N
RW_DECAY_RANK = 64
RW_ICLR_RANK = 64
RW_GATE_RANK = 128
RW_COLS = 3 * RW_W + RW_DECAY_RANK + RW_ICLR_RANK + RW_GATE_RANK
RW_LN_EPS = 64e-5
RW_CHUNK = 64
RW_CHUNKS_PER_STEP = 4
SSD_HEADS = 8
SSD_P = 64
SSD_W = SSD_HEADS * SSD_P
SSD_GROUPS = 2
SSD_STATE = 128
SSD_CONV = 4
SSD_CHUNK = 128
SSD_CHUNKS_PER_STEP = 2
SSD_XBC = SSD_W + 2 * SSD_GROUPS * SSD_STATE
SSD_NORM_EPS = 1e-5
SSD_GW = SSD_W // SSD_GROUPS
S5_GROUP = 16
S5_GROUPS = 32
S5_W = S5_GROUP * S5_GROUPS
S5_STATE = 64
S5_SN = S5_GROUPS * S5_STATE
N_BRANCHES = 3
FFN_HIDDEN = 2816
NORM_EPS = 1e-6
LANES = 128
DT_PAD = LANES

VMEM_LIMIT = 56 * 1024 * 1024


def _cparams(sem):
    return pltpu.CompilerParams(dimension_semantics=sem, vmem_limit_bytes=VMEM_LIMIT)


def _const_spec(arr):
    nd = arr.ndim
    return pl.BlockSpec(arr.shape, lambda *_: (0,) * nd, pipeline_mode=pl.Buffered(1))


def _layer_spec(arr, l):
    nd = arr.ndim - 1
    return pl.BlockSpec((None,) + arr.shape[1:], lambda *_: (l,) + (0,) * nd,
                        pipeline_mode=pl.Buffered(1))


def _mm(a, b):
    return jnp.dot(a.astype(bf16), b.astype(bf16), preferred_element_type=f32)


def _mm_nt(a, b):
    return lax.dot_general(a.astype(bf16), b.astype(bf16), (((1,), (1,)), ((), ())),
                           preferred_element_type=f32)


def _mm_tn(a, b):
    return lax.dot_general(a.astype(bf16), b.astype(bf16), (((0,), (0,)), ((), ())),
                           preferred_element_type=f32)


def _split(x, parts):
    out = []
    for _ in range(parts - 1):
        hi = x.astype(bf16)
        out.append(hi)
        x = x - hi.astype(f32)
    out.append(x.astype(bf16))
    return out


def _mm_exact_rhs(a, b_bf16, parts):
    acc = None
    for piece in _split(a, parts):
        t = jnp.dot(piece, b_bf16, preferred_element_type=f32)
        acc = t if acc is None else acc + t
    return acc


def _mm_exact_lhs(a_bf16, b, parts):
    acc = None
    for piece in _split(b, parts):
        t = jnp.dot(a_bf16, piece, preferred_element_type=f32)
        acc = t if acc is None else acc + t
    return acc


def _rms(x, g, eps):
    return x * lax.rsqrt(jnp.mean(x * x, axis=-1, keepdims=True) + eps) * g


def _sigmoid(x):
    return 1.0 / (1.0 + jnp.exp(-x))


def _softplus(x):
    return jnp.maximum(x, 0.0) + jnp.log(1.0 + jnp.exp(-jnp.abs(x)))


def _silu(x):
    return x * _sigmoid(x)


def _inproj_kernel(x_ref, g_ref, wa_ref, wu_ref, wdt_ref, p_ref, z_ref, xbc_ref, u_ref, dt_ref):
    hb = _rms(x_ref[...], g_ref[...], NORM_EPS).astype(bf16)
    o = 0
    for ref, width in ((p_ref, RW_COLS), (z_ref, SSD_W), (xbc_ref, SSD_XBC)):
        ref[...] = jnp.dot(hb, wa_ref[:, o:o + width], preferred_element_type=f32)
        o += width
    u_ref[...] = jnp.dot(hb, wu_ref[...], preferred_element_type=f32)
    dt_ref[...] = jnp.dot(hb, wdt_ref[...], preferred_element_type=f32)


def _col_block_spec(arr, l, width, idx):
    return pl.BlockSpec((None, arr.shape[1], width), lambda *_: (l, 0, idx), pipeline_mode=pl.Buffered(1))


def _inproj(x, g, w_a, w_ug, w_dt, l, tm):
    B, T, D = x.shape
    nt = T // tm
    row = lambda width: pl.BlockSpec((None, tm, width), lambda b, j: (b, j, 0))
    return pl.pallas_call(
        _inproj_kernel,
        grid=(B, nt),
        in_specs=[row(D), _layer_spec(g, l), _layer_spec(w_a, l), _col_block_spec(w_ug, l, S5_W, 0),
                  _layer_spec(w_dt, l)],
        out_specs=[row(RW_COLS), row(SSD_W), row(SSD_XBC), row(S5_W), row(DT_PAD)],
        out_shape=[jax.ShapeDtypeStruct((B, T, RW_COLS), f32),
                   jax.ShapeDtypeStruct((B, T, SSD_W), f32),
                   jax.ShapeDtypeStruct((B, T, SSD_XBC), f32),
                   jax.ShapeDtypeStruct((B, T, S5_W), f32),
                   jax.ShapeDtypeStruct((B, T, DT_PAD), f32)],
        compiler_params=_cparams(("parallel", "parallel")),
        name="inproj",
    )(x, g, w_a, w_ug, w_dt)


def _stack_heads(x):
    lane = lax.broadcasted_iota(jnp.int32, x.shape, 1)
    lo = lane < RW_N
    return jnp.concatenate([jnp.where(lo, x, 0.0), jnp.where(lo, 0.0, x)], axis=0)


def _head_sums(x, ones_pair):
    R = x.shape[0]
    PW = 2 * RW_N
    st = jnp.concatenate([x[:, q * PW:(q + 1) * PW] for q in range(RW_W // PW)], axis=0)
    s = _mm_exact_rhs(st, ones_pair, 2)
    return jnp.concatenate([s[q * R:(q + 1) * R] for q in range(RW_W // PW)], axis=1)


def _rwkv_kernel(p_ref, mu_ref, w0a0_ref, wwa_ref, gup_ref, kk_ref, ka_ref, rk_ref, lng_ref, lnb_ref,
                 ones_ref, tril_ref, o_ref, prev_ref, h_ref, *, nc):
    C = RW_CHUNK
    TB = nc * C
    PW = 2 * RW_N
    NP = RW_HEADS // 2

    @pl.when(pl.program_id(1) == 0)
    def _():
        prev_ref[...] = jnp.zeros_like(prev_ref)
        h_ref[...] = jnp.zeros_like(h_ref)

    p = p_ref[...]
    row = lax.broadcasted_iota(jnp.int32, p.shape, 0)
    shifted = jnp.where(row == 0, prev_ref[...], pltpu.roll(p, 1, 0))
    prev_ref[...] = p[TB - 1:TB, :]
    pm = p + (shifted - p) * mu_ref[...]
    r = pm[:, 0:RW_W]
    k = pm[:, RW_W:2 * RW_W]
    v = pm[:, 2 * RW_W:3 * RW_W]
    wa_in = pm[:, 3 * RW_W:3 * RW_W + PW]
    gd = pm[:, 3 * RW_W + PW:RW_COLS]
    lane = lax.broadcasted_iota(jnp.int32, wa_in.shape, 1)
    wa_in = jnp.where(lane < RW_DECAY_RANK, jnp.tanh(wa_in), wa_in)
    wa = w0a0_ref[...] + _mm(wa_in, wwa_ref[...])
    w_log = -_softplus(-wa[:, :RW_W]) - 0.5
    lw = -jnp.exp(w_log)
    a = _sigmoid(wa[:, RW_W:])
    g = _mm(_sigmoid(gd), gup_ref[...])

    ones = ones_ref[...]
    kk = k * kk_ref[...]
    kk = kk * lax.rsqrt(jnp.maximum(_head_sums(kk * kk, ones), 1e-24))
    k2 = k * (1.0 + (a - 1.0) * ka_ref[...])

    lc = _mm_exact_lhs(tril_ref[...], lw, 2)
    lc_last = jnp.concatenate(
        [jnp.broadcast_to(lc[(c + 1) * C - 1:(c + 1) * C, :], (C, RW_W)) for c in range(nc)], axis=0)
    e_pos = jnp.exp(lc)
    e_neg = jnp.exp(-lc)
    e_end = jnp.exp(lc_last - lc)
    kka = kk * a
    r_hat = r * e_pos
    a_hat = -kk * jnp.exp(lc - lw)
    b_hat = kka * e_neg
    k_hat = k2 * e_neg
    b_til = kka * e_end
    k_til = k2 * e_end

    ri = lax.broadcasted_iota(jnp.int32, (4 * C, 4 * C), 0)
    ci = lax.broadcasted_iota(jnp.int32, (4 * C, 4 * C), 1)
    tri = (ri % C) + jnp.where(ri < 2 * C, 0, 1) > (ci % C)
    ei = lax.broadcasted_iota(jnp.int32, (PW, PW), 0)
    ej = lax.broadcasted_iota(jnp.int32, (PW, PW), 1)
    eye = ei == ej
    blk = (ei // RW_N) == (ej // RW_N)
    eye_f = jnp.where(eye, 1.0, 0.0)

    inst = [(c, q) for c in range(nc) for q in range(NP)]
    every = range(len(inst))

    def stacked(t):
        return [_stack_heads(t[c * C:(c + 1) * C, q * PW:(q + 1) * PW]) for c, q in inst]

    am, rm, bm, km = stacked(a_hat), stacked(r_hat), stacked(b_hat), stacked(k_hat)
    btm, ktm, vm = stacked(b_til), stacked(k_til), stacked(v)
    a_all = [_mm_nt(jnp.concatenate([am[i], rm[i]], axis=0), jnp.concatenate([bm[i], km[i]], axis=0))
             for i in every]
    a_all = [jnp.where(tri, t, 0.0) for t in a_all]
    pw = [t[:2 * C, :2 * C] for t in a_all]
    t_inv = [eye_f + t for t in pw]
    pw = [_mm(t, t) for t in pw]
    for _ in range(4):
        both = [_mm(jnp.concatenate([pw[i], t_inv[i]], axis=0), pw[i]) for i in every]
        t_inv = [t_inv[i] + both[i][2 * C:] for i in every]
        pw = [t[:2 * C] for t in both]
    t_inv = [t_inv[i] + _mm(t_inv[i], pw[i]) for i in every]
    av = [_mm(a_all[i][:, 2 * C:], vm[i]) for i in every]
    pq = [_mm(t_inv[i], jnp.concatenate([am[i], av[i][:2 * C]], axis=1)) for i in every]
    xy = [_mm(a_all[i][2 * C:, :2 * C], pq[i]) for i in every]
    mk = [_mm_tn(btm[i], pq[i][:, :PW]) for i in every]
    gk = [_mm_tn(jnp.concatenate([btm[i], ktm[i]], axis=0), jnp.concatenate([pq[i][:, PW:], vm[i]], axis=0))
          for i in every]
    mx = [jnp.concatenate([mk[i], rm[i] + xy[i][:, :PW]], axis=0) for i in every]
    ym = [xy[i][:, PW:] + av[i][2 * C:] for i in every]
    gam_col = [jnp.sum(jnp.where(eye, jnp.exp(lc[(c + 1) * C - 1:(c + 1) * C, q * PW:(q + 1) * PW]), 0.0),
                       axis=1, keepdims=True) for c, q in inst]

    hs = [h_ref[q] for q in range(NP)]
    o_rows = []
    for c in range(nc):
        seq = [_mm(mx[c * NP + q], hs[q]) for q in range(NP)]
        hs = [jnp.where(blk, gam_col[c * NP + q] * hs[q] + seq[q][:PW] + gk[c * NP + q], 0.0) for q in range(NP)]
        om = [seq[q][PW:] + ym[c * NP + q] for q in range(NP)]
        o_rows.append(jnp.concatenate([t[:C] + t[C:] for t in om], axis=1))
    for q in range(NP):
        h_ref[q] = hs[q]
    o = jnp.concatenate(o_rows, axis=0)

    inv_n = 1.0 / RW_N
    mean = _head_sums(o, ones) * inv_n
    oc = o - mean
    var = _head_sums(oc * oc, ones) * inv_n
    o = oc * lax.rsqrt(var + RW_LN_EPS) * lng_ref[...] + lnb_ref[...]
    bonus = _head_sums(r * k2 * rk_ref[...], ones) * v
    o_ref[...] = (o + bonus) * g


def _rwkv(p, layer_consts, shared_consts, l, nc):
    B, T, _ = p.shape
    TB = nc * RW_CHUNK
    return pl.pallas_call(
        functools.partial(_rwkv_kernel, nc=nc),
        grid=(B, T // TB),
        in_specs=[pl.BlockSpec((None, TB, RW_COLS), lambda b, j: (b, j, 0))]
                 + [_layer_spec(c, l) for c in layer_consts] + [_const_spec(c) for c in shared_consts],
        out_specs=pl.BlockSpec((None, TB, RW_W), lambda b, j: (b, j, 0)),
        out_shape=jax.ShapeDtypeStruct((B, T, RW_W), f32),
        scratch_shapes=[pltpu.VMEM((1, RW_COLS), f32),
                        pltpu.VMEM((RW_HEADS // 2, 2 * RW_N, 2 * RW_N), f32)],
        compiler_params=_cparams(("parallel", "arbitrary")),
        name="rwkv7",
    )(p, *layer_consts, *shared_consts)


def _ssd_kernel(z_ref, xbc_ref, dt_ref, cw_ref, cb_ref, dtb_ref, alog_ref, dskip_ref, ng_ref,
                exp_ref, tril_ref, o_ref, hist_ref, st_ref, *, nc):
    L = SSD_CHUNK
    TB = nc * L
    HIST = 8

    @pl.when(pl.program_id(1) == 0)
    def _():
        hist_ref[...] = jnp.zeros_like(hist_ref)
        st_ref[...] = jnp.zeros_like(st_ref)

    x_in = xbc_ref[...]
    hist = hist_ref[...]
    row8 = lax.broadcasted_iota(jnp.int32, (HIST, SSD_XBC), 0)
    conv = cb_ref[...] + cw_ref[SSD_CONV - 1:SSD_CONV, :] * x_in
    for back in range(1, SSD_CONV):
        rolled = pltpu.roll(x_in, back, 0)
        head = jnp.where(row8 < back, pltpu.roll(hist, back, 0), rolled[0:HIST])
        shifted = jnp.concatenate([head, rolled[HIST:]], axis=0)
        conv = conv + cw_ref[SSD_CONV - 1 - back:SSD_CONV - back, :] * shifted
    hist_ref[...] = x_in[TB - HIST:TB, :]
    xbc = _silu(conv)
    xs = xbc[:, :SSD_W]
    bmat = xbc[:, SSD_W:SSD_W + SSD_GROUPS * SSD_STATE]
    cmat = xbc[:, SSD_W + SSD_GROUPS * SSD_STATE:]

    dt = _softplus(dt_ref[...] + dtb_ref[...])
    da = dt * -jnp.exp(alog_ref[...])
    acs = _mm_exact_lhs(tril_ref[...], da, 2)
    rep = _mm_exact_rhs(jnp.concatenate([dt, acs], axis=0), exp_ref[...], 2)
    dt_rep = rep[:TB]
    acs_rep = rep[TB:]

    lane = lax.broadcasted_iota(jnp.int32, (TB, LANES), 1)
    lo = lane < SSD_P

    def pair64(rep, q):
        return jnp.where(lo, rep[:, (2 * q) * LANES:(2 * q + 1) * LANES],
                         rep[:, (2 * q + 1) * LANES:(2 * q + 2) * LANES])

    dt64 = jnp.concatenate([pair64(dt_rep, q) for q in range(SSD_HEADS // 2)], axis=1)
    acs64 = jnp.concatenate([pair64(acs_rep, q) for q in range(SSD_HEADS // 2)], axis=1)
    xdt = xs * dt64
    last = [acs64[(c + 1) * L - 1:(c + 1) * L, :] for c in range(nc)]
    acs_last = jnp.concatenate([jnp.broadcast_to(t, (L, SSD_W)) for t in last], axis=0)
    xdec = xdt * jnp.exp(acs_last - acs64)
    e_acs = jnp.exp(acs64)
    chunk_decay = [jnp.exp(t) for t in last]

    li = lax.broadcasted_iota(jnp.int32, (L, L), 0)
    si = lax.broadcasted_iota(jnp.int32, (L, L), 1)
    causal = li >= si
    eye = li == si
    glane = lax.broadcasted_iota(jnp.int32, (L, SSD_GW), 1)
    hpg = SSD_HEADS // SSD_GROUPS
    G = SSD_GROUPS
    inst = [(c, gi) for c in range(nc) for gi in range(G)]
    every = range(len(inst))
    rows = lambda c: slice(c * L, (c + 1) * L)
    gsl = lambda gi: slice(gi * SSD_GW, (gi + 1) * SSD_GW)
    ssl = lambda gi: slice(gi * SSD_STATE, (gi + 1) * SSD_STATE)
    bg = [bmat[rows(c), ssl(gi)] for c, gi in inst]
    cg = [cmat[rows(c), ssl(gi)] for c, gi in inst]
    scores = [_mm_nt(cg[i], bg[i]) for i in every]
    wcat, xst = [], []
    for i, (c, gi) in enumerate(inst):
        xg = xdt[rows(c), gsl(gi)]
        wm, xm = [], []
        for j in range(hpg):
            hd = gi * hpg + j
            col = acs_rep[rows(c), hd * LANES:(hd + 1) * LANES]
            rowv = jnp.sum(jnp.where(eye, col, 0.0), axis=0, keepdims=True)
            wm.append(jnp.where(causal, scores[i] * jnp.exp(col - rowv), 0.0))
            xm.append(jnp.where((glane // SSD_P) == j, xg, 0.0))
        wcat.append(jnp.concatenate(wm, axis=1))
        xst.append(jnp.concatenate(xm, axis=0))
    y_diag = [_mm(wcat[i], xst[i]) for i in every]
    upd = [_mm_tn(bg[i], xdec[rows(c), gsl(gi)]) for i, (c, gi) in enumerate(inst)]
    st = [st_ref[gi] for gi in range(G)]
    y_rows = []
    for c in range(nc):
        y_off = [_mm(cg[c * G + gi], st[gi]) * e_acs[rows(c), gsl(gi)] for gi in range(G)]
        st = [chunk_decay[c][:, gsl(gi)] * st[gi] + upd[c * G + gi] for gi in range(G)]
        y_rows.append(jnp.concatenate([y_diag[c * G + gi] + y_off[gi] for gi in range(G)], axis=1))
    for gi in range(G):
        st_ref[gi] = st[gi]
    y = jnp.concatenate(y_rows, axis=0) + xs * dskip_ref[...]
    y = y * _silu(z_ref[...])
    outs = []
    for gi in range(SSD_GROUPS):
        yg = y[:, gi * SSD_GW:(gi + 1) * SSD_GW]
        outs.append(yg * lax.rsqrt(jnp.mean(yg * yg, axis=-1, keepdims=True) + SSD_NORM_EPS))
    o_ref[...] = jnp.concatenate(outs, axis=1) * ng_ref[...]


def _ssd(z, xbc, dt, layer_consts, shared_consts, l, nc):
    B, T, _ = z.shape
    L = nc * SSD_CHUNK
    row = lambda width: pl.BlockSpec((None, L, width), lambda b, j: (b, j, 0))
    return pl.pallas_call(
        functools.partial(_ssd_kernel, nc=nc),
        grid=(B, T // L),
        in_specs=[row(SSD_W), row(SSD_XBC), row(DT_PAD)]
                 + [_layer_spec(c, l) for c in layer_consts] + [_const_spec(c) for c in shared_consts],
        out_specs=row(SSD_W),
        out_shape=jax.ShapeDtypeStruct((B, T, SSD_W), f32),
        scratch_shapes=[pltpu.VMEM((8, SSD_XBC), f32),
                        pltpu.VMEM((SSD_GROUPS, SSD_STATE, SSD_GW), f32)],
        compiler_params=_cparams(("parallel", "arbitrary")),
        name="ssd",
    )(z, xbc, dt, *layer_consts, *shared_consts)


def _s5_kernel(u_ref, lre_ref, lim_ref, lstep_ref, wb_ref, cre_ref, cim_ref, d_ref,
               gw_ref, gb_ref, perm_ref, permt_ref, o_ref, sre_ref, sim_ref, bur_ref, bui_ref, par_ref,
               *, nb, tt):
    @pl.when(pl.program_id(0) == 0)
    def _():
        sre_ref[...] = jnp.zeros_like(sre_ref)
        sim_ref[...] = jnp.zeros_like(sim_ref)
        lr = lre_ref[...]
        li = lim_ref[...]
        step = jnp.exp(lstep_ref[...])
        mag = jnp.exp(lr * step)
        a_re = mag * jnp.cos(li * step)
        a_im = mag * jnp.sin(li * step)
        den = lr * lr + li * li
        par_ref[0:1, :] = a_re
        par_ref[1:2, :] = a_im
        par_ref[2:3, :] = ((a_re - 1.0) * lr + a_im * li) / den
        par_ref[3:4, :] = (a_im * lr - (a_re - 1.0) * li) / den

    ab_re = par_ref[0:1, :]
    ab_im = par_ref[1:2, :]
    coef_re = par_ref[2:3, :]
    coef_im = par_ref[3:4, :]

    ub = jnp.dot(perm_ref[...], u_ref[...].reshape(nb * tt, S5_W).astype(bf16),
                 preferred_element_type=f32).astype(bf16)
    u = ub.astype(f32)
    in_per_tile = LANES // S5_STATE * S5_GROUP
    for j in range(S5_SN // LANES):
        us = slice(j * in_per_tile // LANES * LANES, (j * in_per_tile // LANES + 1) * LANES)
        ls = slice(j * LANES, (j + 1) * LANES)
        xri = jnp.dot(ub[:, us], wb_ref[us, 2 * j * LANES:2 * (j + 1) * LANES], preferred_element_type=f32)
        xr = xri[:, :LANES]
        xi = xri[:, LANES:]
        bur_ref[:, ls] = coef_re[:, ls] * xr - coef_im[:, ls] * xi
        bui_ref[:, ls] = coef_re[:, ls] * xi + coef_im[:, ls] * xr

    LB = 4 * LANES
    for jb in range(S5_SN // LB):
        ls = slice(jb * LB, (jb + 1) * LB)
        ar = jnp.broadcast_to(ab_re[:, ls], (nb, LB))
        ai = jnp.broadcast_to(ab_im[:, ls], (nb, LB))

        def body(t, carry):
            s_re, s_im = carry
            rows = pl.ds(pl.multiple_of(t * nb, nb), nb)
            n_re = ar * s_re - ai * s_im + bur_ref[rows, ls]
            n_im = ar * s_im + ai * s_re + bui_ref[rows, ls]
            bur_ref[rows, ls] = n_re
            bui_ref[rows, ls] = n_im
            return n_re, n_im

        s_re, s_im = lax.fori_loop(0, tt, body, (sre_ref[:, ls], sim_ref[:, ls]), unroll=2)
        sre_ref[:, ls] = s_re
        sim_ref[:, ls] = s_im

    st_per_tile = LANES // S5_GROUP * S5_STATE
    ys = []
    for m in range(S5_W // LANES):
        ks = slice(m * st_per_tile, (m + 1) * st_per_tile)
        os_ = slice(m * LANES, (m + 1) * LANES)
        ys.append(jnp.dot(bur_ref[:, ks].astype(bf16), cre_ref[ks, os_], preferred_element_type=f32)
                  - jnp.dot(bui_ref[:, ks].astype(bf16), cim_ref[ks, os_], preferred_element_type=f32))
    y = jnp.concatenate(ys, axis=1) + d_ref[...] * u
    zg = 0.5 * y * (1.0 + jnp.tanh(0.7978845608028654 * (y + 0.044715 * (y * y * y))))
    out = (zg * _sigmoid(_mm(zg, gw_ref[...]) + gb_ref[...])).astype(bf16)
    o_ref[...] = jnp.dot(permt_ref[...], out, preferred_element_type=f32).astype(bf16).reshape(nb, tt, S5_W)


def _s5(u, layer_consts, l, tt):
    B, T, _ = u.shape
    blk = tt * B
    src = jnp.arange(blk)
    perm = (((src % B) * tt + src // B)[:, None] == src[None, :]).astype(bf16)
    shared = (perm, perm.T)
    spec = pl.BlockSpec((B, tt, S5_W), lambda i: (0, i, 0))
    return pl.pallas_call(
        functools.partial(_s5_kernel, nb=B, tt=tt),
        grid=(T // tt,),
        in_specs=[spec] + [_layer_spec(c, l) for c in layer_consts] + [_const_spec(c) for c in shared],
        out_specs=spec,
        out_shape=jax.ShapeDtypeStruct((B, T, S5_W), bf16),
        scratch_shapes=[pltpu.VMEM((B, S5_SN), f32), pltpu.VMEM((B, S5_SN), f32),
                        pltpu.VMEM((blk, S5_SN), f32), pltpu.VMEM((blk, S5_SN), f32),
                        pltpu.VMEM((8, S5_SN), f32)],
        compiler_params=_cparams(("arbitrary",)),
        name="s5",
    )(u, *layer_consts, *shared)


GATE_BLOCKS = N_BRANCHES * D_MODEL // S5_W


def _merge_kernel(x_ref, oa_ref, ob_ref, oc_ref, g_ref, *rest):
    wg_refs = rest[:GATE_BLOCKS]
    pa_ref, pb_ref, pc_ref, wo_ref, out_ref = rest[GATE_BLOCKS:]
    per = GATE_BLOCKS // N_BRANCHES
    x = x_ref[...]
    hb = _rms(x, g_ref[...], NORM_EPS).astype(bf16)
    merged = None
    for i, (o_ref, p_ref) in enumerate(((oa_ref, pa_ref), (ob_ref, pb_ref), (oc_ref, pc_ref))):
        logits = jnp.concatenate([jnp.dot(hb, w[...], preferred_element_type=f32)
                                  for w in wg_refs[i * per:(i + 1) * per]], axis=1)
        term = _sigmoid(logits) * jnp.dot(o_ref[...].astype(bf16), p_ref[...], preferred_element_type=f32)
        merged = term if merged is None else merged + term
    out_ref[...] = x + jnp.dot(merged.astype(bf16), wo_ref[...], preferred_element_type=f32)


def _merge(x, oa, ob, oc, g, w_ug, layer_consts, l, tm):
    B, T, D = x.shape
    row = lambda width: pl.BlockSpec((None, tm, width), lambda b, j: (b, j, 0))
    gate_specs = [_col_block_spec(w_ug, l, S5_W, 1 + k) for k in range(GATE_BLOCKS)]
    return pl.pallas_call(
        _merge_kernel,
        grid=(B, T // tm),
        in_specs=[row(D), row(RW_W), row(SSD_W), row(S5_W), _layer_spec(g, l)] + gate_specs
                 + [_layer_spec(c, l) for c in layer_consts],
        out_specs=row(D),
        out_shape=jax.ShapeDtypeStruct((B, T, D), f32),
        compiler_params=_cparams(("parallel", "parallel")),
        name="merge",
    )(x, oa, ob, oc, g, *([w_ug] * GATE_BLOCKS), *layer_consts)


def _ffn_kernel(x_ref, g_ref, w1_ref, w2_ref, fg_ref, out_ref, *, final):
    x = x_ref[...]
    hb = _rms(x, g_ref[...], NORM_EPS).astype(bf16)
    gate = jnp.dot(hb, w1_ref[:, :FFN_HIDDEN], preferred_element_type=f32)
    up = jnp.dot(hb, w1_ref[:, FFN_HIDDEN:], preferred_element_type=f32)
    act = (_silu(gate) * up).astype(bf16)
    y = x + jnp.dot(act, w2_ref[...], preferred_element_type=f32)
    if final:
        y = _rms(y, fg_ref[...], NORM_EPS)
    out_ref[...] = y


def _ffn(x, layer_consts, fg, l, tm, final):
    B, T, D = x.shape
    row = pl.BlockSpec((None, tm, D), lambda b, j: (b, j, 0))
    return pl.pallas_call(
        functools.partial(_ffn_kernel, final=final),
        grid=(B, T // tm),
        in_specs=[row] + [_layer_spec(c, l) for c in layer_consts] + [_const_spec(fg)],
        out_specs=row,
        out_shape=jax.ShapeDtypeStruct((B, T, D), f32),
        compiler_params=_cparams(("parallel", "parallel")),
        name="ffn",
    )(x, *layer_consts, fg)


W_IN_A = RW_COLS + SSD_W + SSD_XBC
W_IN_UG = W_IN_A + SSD_HEADS
W_IN_COLS = W_IN_UG + S5_W + N_BRANCHES * D_MODEL


def _win_split_kernel(w_ref, wa_ref, wug_ref, wdt_ref):
    w = w_ref[...]
    wa_ref[...] = w[:, :W_IN_A].astype(bf16)
    wug_ref[...] = w[:, W_IN_UG:].astype(bf16)
    tile = w[:, W_IN_A:W_IN_A + DT_PAD]
    lane = lax.broadcasted_iota(jnp.int32, tile.shape, 1)
    wdt_ref[...] = jnp.where(lane < SSD_HEADS, tile, 0.0).astype(bf16)


def _win_split(w_in, rb):
    depth, D, cols = w_in.shape
    spec = lambda width: pl.BlockSpec((None, rb, width), lambda l, i: (l, i, 0))
    return pl.pallas_call(
        _win_split_kernel,
        grid=(depth, D // rb),
        in_specs=[spec(cols)],
        out_specs=[spec(W_IN_A), spec(cols - W_IN_UG), spec(DT_PAD)],
        out_shape=[jax.ShapeDtypeStruct((depth, D, W_IN_A), bf16),
                   jax.ShapeDtypeStruct((depth, D, cols - W_IN_UG), bf16),
                   jax.ShapeDtypeStruct((depth, D, DT_PAD), bf16)],
        compiler_params=_cparams(("parallel", "parallel")),
        name="w_in_split",
    )(w_in)


def _rows(v):
    return v.reshape(v.shape[0], 1, -1).astype(f32)


def _s5_block_diag(w2d, row_group, col_group):
    tiled = jnp.tile(w2d, (1, 1, S5_GROUPS))
    r = jnp.arange(tiled.shape[1]) // row_group
    c = jnp.arange(tiled.shape[2]) // col_group
    return jnp.where((r[:, None] == c[None, :])[None], tiled, 0.0).astype(bf16)


def _tile_tm(T):
    for tm in (512, 256, 128):
        if T % tm == 0:
            return tm
    raise ValueError("sequence length must be a multiple of 128")


def kernel(x, norm_mix, w_in, rwkv_mu, rwkv_w0, rwkv_w_up, rwkv_a0, rwkv_a_up, rwkv_g_up, rwkv_k_k, rwkv_k_a, rwkv_r_k, rwkv_ln_g, rwkv_ln_b, proj_a, ssd_conv_w, ssd_conv_b, ssd_dt_bias, ssd_a_log, ssd_d, ssd_norm_g, proj_b, s5_lam_re, s5_lam_im, s5_log_step, s5_b_re, s5_b_im, s5_c_re, s5_c_im, s5_d, s5_glu_w, s5_glu_b, proj_c, w_out, norm_ffn, ffn_w_in, ffn_w_out, final_norm):
    B, T, D = x.shape
    depth = w_in.shape[0]
    rw_nc = RW_CHUNKS_PER_STEP
    assert D == D_MODEL and T % SSD_CHUNK == 0 and B % 8 == 0 and T % (rw_nc * RW_CHUNK) == 0
    tm = _tile_tm(T)
    hid = jnp.arange(2 * RW_N) // RW_N
    head_ones = (hid[:, None] == hid[None, :]).astype(bf16)
    c_idx = jnp.arange(rw_nc * RW_CHUNK)
    tril_rw = ((c_idx[:, None] >= c_idx[None, :])
               & (c_idx[:, None] // RW_CHUNK == c_idx[None, :] // RW_CHUNK)).astype(bf16)
    ssd_nc = SSD_CHUNKS_PER_STEP if T % (SSD_CHUNKS_PER_STEP * SSD_CHUNK) == 0 else 1
    l_idx = jnp.arange(ssd_nc * SSD_CHUNK)
    tril_ssd = ((l_idx[:, None] >= l_idx[None, :])
                & (l_idx[:, None] // SSD_CHUNK == l_idx[None, :] // SSD_CHUNK)).astype(bf16)
    expand = (jnp.arange(DT_PAD)[:, None] == (jnp.arange(SSD_HEADS * LANES) // LANES)[None, :]).astype(bf16)

    g_mix = _rows(norm_mix)
    assert w_in.shape[2] == W_IN_COLS
    w_a, w_ug, w_dt = _win_split(w_in, 256)
    zr = jnp.zeros((depth, RW_DECAY_RANK, RW_W), f32)
    w_wa = jnp.concatenate([jnp.concatenate([rwkv_w_up, zr], axis=2),
                            jnp.concatenate([zr, rwkv_a_up], axis=2)], axis=1).astype(bf16)
    rw_layer = (_rows(rwkv_mu), jnp.concatenate([_rows(rwkv_w0), _rows(rwkv_a0)], axis=2), w_wa,
                rwkv_g_up.astype(bf16), _rows(rwkv_k_k), _rows(rwkv_k_a), _rows(rwkv_r_k),
                _rows(rwkv_ln_g), _rows(rwkv_ln_b))
    ssd_layer = (
        jnp.concatenate([ssd_conv_w, jnp.zeros((depth, 8 - SSD_CONV, SSD_XBC), f32)], axis=1),
        _rows(ssd_conv_b),
        jnp.concatenate([_rows(ssd_dt_bias), jnp.zeros((depth, 1, DT_PAD - SSD_HEADS), f32)], axis=2),
        jnp.concatenate([_rows(ssd_a_log), jnp.zeros((depth, 1, DT_PAD - SSD_HEADS), f32)], axis=2),
        _rows(jnp.repeat(ssd_d, SSD_P, axis=1)),
        _rows(ssd_norm_g))
    b_rows = lambda b: jnp.swapaxes(b, 2, 3).reshape(depth, S5_W, S5_STATE)
    c_rows = lambda c: jnp.swapaxes(c, 2, 3).reshape(depth, S5_SN, S5_GROUP)
    nlt = S5_SN // LANES
    w_b = jnp.stack([_s5_block_diag(b_rows(s5_b_re), S5_GROUP, S5_STATE).reshape(depth, S5_W, nlt, LANES),
                     _s5_block_diag(b_rows(s5_b_im), S5_GROUP, S5_STATE).reshape(depth, S5_W, nlt, LANES)],
                    axis=3).reshape(depth, S5_W, 2 * S5_SN)
    s5_layer = (
        _rows(s5_lam_re), _rows(s5_lam_im), _rows(jnp.repeat(s5_log_step, S5_STATE, axis=1)),
        w_b,
        _s5_block_diag(c_rows(s5_c_re), S5_STATE, S5_GROUP),
        _s5_block_diag(c_rows(s5_c_im), S5_STATE, S5_GROUP),
        _rows(s5_d), s5_glu_w.astype(bf16), _rows(s5_glu_b))
    merge_layer = (proj_a.astype(bf16), proj_b.astype(bf16), proj_c.astype(bf16), w_out.astype(bf16))
    ffn_layer = (_rows(norm_ffn), ffn_w_in.astype(bf16), ffn_w_out.astype(bf16))
    fg = final_norm.reshape(1, D).astype(f32)
    s5_tt = 32 if T % 32 == 0 else 8

    for l in range(depth):
        p, z, xbc, u, dt = _inproj(x, g_mix, w_a, w_ug, w_dt, l, tm)
        oa = _rwkv(p, rw_layer, (head_ones, tril_rw), l, rw_nc)
        ob = _ssd(z, xbc, dt, ssd_layer, (expand, tril_ssd), l, ssd_nc)
        oc = _s5(u, s5_layer, l, s5_tt)
        x = _merge(x, oa, ob, oc, g_mix, w_ug, merge_layer, l, tm)
        x = _ffn(x, ffn_layer, fg, l, tm, final=(l == depth - 1))
    return x
```

```python
import functools

import jax
import jax.numpy as jnp
from jax import lax
from jax.experimental import pallas as pl
from jax.experimental.pallas import tpu as pltpu

f32 = jnp.float32
bf16 = jnp.bfloat16

D_MODEL = 1024
RW_HEADS = 8
RW_N = 64
RW_W = RW_HEADS * RW_N
RW_DECAY_RANK = 64
RW_ICLR_RANK = 64
RW_GATE_RANK = 128
RW_COLS = 3 * RW_W + RW_DECAY_RANK + RW_ICLR_RANK + RW_GATE_RANK
RW_LN_EPS = 64e-5
RW_CHUNK = 64
RW_CHUNKS_PER_STEP = 4
SSD_HEADS = 8
SSD_P = 64
SSD_W = SSD_HEADS * SSD_P
SSD_GROUPS = 2
SSD_STATE = 128
SSD_CONV = 4
SSD_CHUNK = 128
SSD_XBC = SSD_W + 2 * SSD_GROUPS * SSD_STATE
SSD_NORM_EPS = 1e-5
SSD_GW = SSD_W // SSD_GROUPS
S5_GROUP = 16
S5_GROUPS = 32
S5_W = S5_GROUP * S5_GROUPS
S5_STATE = 64
S5_SN = S5_GROUPS * S5_STATE
N_BRANCHES = 3
FFN_HIDDEN = 2816
NORM_EPS = 1e-6
LANES = 128
DT_PAD = LANES

VMEM_LIMIT = 56 * 1024 * 1024


def _cparams(sem):
    return pltpu.CompilerParams(dimension_semantics=sem, vmem_limit_bytes=VMEM_LIMIT)


def _const_spec(arr):
    nd = arr.ndim
    return pl.BlockSpec(arr.shape, lambda *_: (0,) * nd, pipeline_mode=pl.Buffered(1))


def _layer_spec(arr, l):
    nd = arr.ndim - 1
    return pl.BlockSpec((None,) + arr.shape[1:], lambda *_: (l,) + (0,) * nd,
                        pipeline_mode=pl.Buffered(1))


def _mm(a, b):
    return jnp.dot(a.astype(bf16), b.astype(bf16), preferred_element_type=f32)


def _mm_nt(a, b):
    return lax.dot_general(a.astype(bf16), b.astype(bf16), (((1,), (1,)), ((), ())),
                           preferred_element_type=f32)


def _mm_tn(a, b):
    return lax.dot_general(a.astype(bf16), b.astype(bf16), (((0,), (0,)), ((), ())),
                           preferred_element_type=f32)


def _split(x, parts):
    out = []
    for _ in range(parts - 1):
        hi = x.astype(bf16)
        out.append(hi)
        x = x - hi.astype(f32)
    out.append(x.astype(bf16))
    return out


def _mm_exact_rhs(a, b_bf16, parts):
    acc = None
    for piece in _split(a, parts):
        t = jnp.dot(piece, b_bf16, preferred_element_type=f32)
        acc = t if acc is None else acc + t
    return acc


def _mm_exact_lhs(a_bf16, b, parts):
    acc = None
    for piece in _split(b, parts):
        t = jnp.dot(a_bf16, piece, preferred_element_type=f32)
        acc = t if acc is None else acc + t
    return acc


def _rms(x, g, eps):
    return x * lax.rsqrt(jnp.mean(x * x, axis=-1, keepdims=True) + eps) * g


def _sigmoid(x):
    return 1.0 / (1.0 + jnp.exp(-x))


def _softplus(x):
    return jnp.maximum(x, 0.0) + jnp.log(1.0 + jnp.exp(-jnp.abs(x)))


def _silu(x):
    return x * _sigmoid(x)


def _inproj_kernel(x_ref, g_ref, wa_ref, wu_ref, wdt_ref, p_ref, z_ref, xbc_ref, u_ref, dt_ref):
    hb = _rms(x_ref[...], g_ref[...], NORM_EPS).astype(bf16)
    o = 0
    for ref, width in ((p_ref, RW_COLS), (z_ref, SSD_W), (xbc_ref, SSD_XBC)):
        ref[...] = jnp.dot(hb, wa_ref[:, o:o + width], preferred_element_type=f32)
        o += width
    u_ref[...] = jnp.dot(hb, wu_ref[...], preferred_element_type=f32)
    dt_ref[...] = jnp.dot(hb, wdt_ref[...], preferred_element_type=f32)


def _col_block_spec(arr, l, width, idx):
    return pl.BlockSpec((None, arr.shape[1], width), lambda *_: (l, 0, idx), pipeline_mode=pl.Buffered(1))


def _inproj(x, g, w_a, w_ug, w_dt, l, tm):
    B, T, D = x.shape
    nt = T // tm
    row = lambda width: pl.BlockSpec((None, tm, width), lambda b, j: (b, j, 0))
    return pl.pallas_call(
        _inproj_kernel,
        grid=(B, nt),
        in_specs=[row(D), _layer_spec(g, l), _layer_spec(w_a, l), _col_block_spec(w_ug, l, S5_W, 0),
                  _layer_spec(w_dt, l)],
        out_specs=[row(RW_COLS), row(SSD_W), row(SSD_XBC), row(S5_W), row(DT_PAD)],
        out_shape=[jax.ShapeDtypeStruct((B, T, RW_COLS), f32),
                   jax.ShapeDtypeStruct((B, T, SSD_W), f32),
                   jax.ShapeDtypeStruct((B, T, SSD_XBC), f32),
                   jax.ShapeDtypeStruct((B, T, S5_W), f32),
                   jax.ShapeDtypeStruct((B, T, DT_PAD), f32)],
        compiler_params=_cparams(("parallel", "parallel")),
        name="inproj",
    )(x, g, w_a, w_ug, w_dt)


def _stack_heads(x):
    lane = lax.broadcasted_iota(jnp.int32, x.shape, 1)
    lo = lane < RW_N
    return jnp.concatenate([jnp.where(lo, x, 0.0), jnp.where(lo, 0.0, x)], axis=0)


def _head_sums(x, ones_pair):
    R = x.shape[0]
    PW = 2 * RW_N
    st = jnp.concatenate([x[:, q * PW:(q + 1) * PW] for q in range(RW_W // PW)], axis=0)
    s = _mm_exact_rhs(st, ones_pair, 2)
    return jnp.concatenate([s[q * R:(q + 1) * R] for q in range(RW_W // PW)], axis=1)


def _rwkv_kernel(p_ref, mu_ref, w0a0_ref, wwa_ref, gup_ref, kk_ref, ka_ref, rk_ref, lng_ref, lnb_ref,
                 ones_ref, tril_ref, o_ref, prev_ref, h_ref, *, nc):
    C = RW_CHUNK
    TB = nc * C
    PW = 2 * RW_N
    NP = RW_HEADS // 2

    @pl.when(pl.program_id(1) == 0)
    def _():
        prev_ref[...] = jnp.zeros_like(prev_ref)
        h_ref[...] = jnp.zeros_like(h_ref)

    p = p_ref[...]
    row = lax.broadcasted_iota(jnp.int32, p.shape, 0)
    shifted = jnp.where(row == 0, prev_ref[...], pltpu.roll(p, 1, 0))
    prev_ref[...] = p[TB - 1:TB, :]
    pm = p + (shifted - p) * mu_ref[...]
    r = pm[:, 0:RW_W]
    k = pm[:, RW_W:2 * RW_W]
    v = pm[:, 2 * RW_W:3 * RW_W]
    wa_in = pm[:, 3 * RW_W:3 * RW_W + PW]
    gd = pm[:, 3 * RW_W + PW:RW_COLS]
    lane = lax.broadcasted_iota(jnp.int32, wa_in.shape, 1)
    wa_in = jnp.where(lane < RW_DECAY_RANK, jnp.tanh(wa_in), wa_in)
    wa = w0a0_ref[...] + _mm(wa_in, wwa_ref[...])
    w_log = -_softplus(-wa[:, :RW_W]) - 0.5
    lw = -jnp.exp(w_log)
    a = _sigmoid(wa[:, RW_W:])
    g = _mm(_sigmoid(gd), gup_ref[...])

    ones = ones_ref[...]
    kk = k * kk_ref[...]
    kk = kk * lax.rsqrt(jnp.maximum(_head_sums(kk * kk, ones), 1e-24))
    k2 = k * (1.0 + (a - 1.0) * ka_ref[...])

    lc = _mm_exact_lhs(tril_ref[...], lw, 2)
    lc_last = jnp.concatenate(
        [jnp.broadcast_to(lc[(c + 1) * C - 1:(c + 1) * C, :], (C, RW_W)) for c in range(nc)], axis=0)
    e_pos = jnp.exp(lc)
    e_neg = jnp.exp(-lc)
    e_end = jnp.exp(lc_last - lc)
    kka = kk * a
    r_hat = r * e_pos
    a_hat = -kk * jnp.exp(lc - lw)
    b_hat = kka * e_neg
    k_hat = k2 * e_neg
    b_til = kka * e_end
    k_til = k2 * e_end

    ri = lax.broadcasted_iota(jnp.int32, (4 * C, 4 * C), 0)
    ci = lax.broadcasted_iota(jnp.int32, (4 * C, 4 * C), 1)
    tri = (ri % C) + jnp.where(ri < 2 * C, 0, 1) > (ci % C)
    ei = lax.broadcasted_iota(jnp.int32, (PW, PW), 0)
    ej = lax.broadcasted_iota(jnp.int32, (PW, PW), 1)
    eye = ei == ej
    blk = (ei // RW_N) == (ej // RW_N)
    eye_f = jnp.where(eye, 1.0, 0.0)

    inst = [(c, q) for c in range(nc) for q in range(NP)]
    every = range(len(inst))

    def stacked(t):
        return [_stack_heads(t[c * C:(c + 1) * C, q * PW:(q + 1) * PW]) for c, q in inst]

    am, rm, bm, km = stacked(a_hat), stacked(r_hat), stacked(b_hat), stacked(k_hat)
    btm, ktm, vm = stacked(b_til), stacked(k_til), stacked(v)
    a_all = [_mm_nt(jnp.concatenate([am[i], rm[i]], axis=0), jnp.concatenate([bm[i], km[i]], axis=0))
             for i in every]
    a_all = [jnp.where(tri, t, 0.0) for t in a_all]
    pw = [t[:2 * C, :2 * C] for t in a_all]
    t_inv = [eye_f + t for t in pw]
    pw = [_mm(t, t) for t in pw]
    for _ in range(4):
        both = [_mm(jnp.concatenate([pw[i], t_inv[i]], axis=0), pw[i]) for i in every]
        t_inv = [t_inv[i] + both[i][2 * C:] for i in every]
        pw = [t[:2 * C] for t in both]
    t_inv = [t_inv[i] + _mm(t_inv[i], pw[i]) for i in every]
    av = [_mm(a_all[i][:, 2 * C:], vm[i]) for i in every]
    pq = [_mm(t_inv[i], jnp.concatenate([am[i], av[i][:2 * C]], axis=1)) for i in every]
    xy = [_mm(a_all[i][2 * C:, :2 * C], pq[i]) for i in every]
    mk = [_mm_tn(btm[i], pq[i][:, :PW]) for i in every]
    gk = [_mm_tn(jnp.concatenate([btm[i], ktm[i]], axis=0), jnp.concatenate([pq[i][:, PW:], vm[i]], axis=0))
          for i in every]
    mx = [jnp.concatenate([mk[i], rm[i] + xy[i][:, :PW]], axis=0) for i in every]
    ym = [xy[i][:, PW:] + av[i][2 * C:] for i in every]
    gam_col = [jnp.sum(jnp.where(eye, jnp.exp(lc[(c + 1) * C - 1:(c + 1) * C, q * PW:(q + 1) * PW]), 0.0),
                       axis=1, keepdims=True) for c, q in inst]

    hs = [h_ref[q] for q in range(NP)]
    o_rows = []
    for c in range(nc):
        seq = [_mm(mx[c * NP + q], hs[q]) for q in range(NP)]
        hs = [jnp.where(blk, gam_col[c * NP + q] * hs[q] + seq[q][:PW] + gk[c * NP + q], 0.0) for q in range(NP)]
        om = [seq[q][PW:] + ym[c * NP + q] for q in range(NP)]
        o_rows.append(jnp.concatenate([t[:C] + t[C:] for t in om], axis=1))
    for q in range(NP):
        h_ref[q] = hs[q]
    o = jnp.concatenate(o_rows, axis=0)

    inv_n = 1.0 / RW_N
    mean = _head_sums(o, ones) * inv_n
    oc = o - mean
    var = _head_sums(oc * oc, ones) * inv_n
    o = oc * lax.rsqrt(var + RW_LN_EPS) * lng_ref[...] + lnb_ref[...]
    bonus = _head_sums(r * k2 * rk_ref[...], ones) * v
    o_ref[...] = (o + bonus) * g


def _rwkv(p, layer_consts, shared_consts, l, nc):
    B, T, _ = p.shape
    TB = nc * RW_CHUNK
    return pl.pallas_call(
        functools.partial(_rwkv_kernel, nc=nc),
        grid=(B, T // TB),
        in_specs=[pl.BlockSpec((None, TB, RW_COLS), lambda b, j: (b, j, 0))]
                 + [_layer_spec(c, l) for c in layer_consts] + [_const_spec(c) for c in shared_consts],
        out_specs=pl.BlockSpec((None, TB, RW_W), lambda b, j: (b, j, 0)),
        out_shape=jax.ShapeDtypeStruct((B, T, RW_W), f32),
        scratch_shapes=[pltpu.VMEM((1, RW_COLS), f32),
                        pltpu.VMEM((RW_HEADS // 2, 2 * RW_N, 2 * RW_N), f32)],
        compiler_params=_cparams(("parallel", "arbitrary")),
        name="rwkv7",
    )(p, *layer_consts, *shared_consts)


def _ssd_stages(z_ref, xbc_ref, dt_ref, cw_ref, cb_ref, dtb_ref, alog_ref, dskip_ref, ng_ref,
                tril_ref, hist_ref, st_ref, result, *, nc):
    L = SSD_CHUNK
    TB = nc * L
    HIST = 8

    @pl.when(pl.program_id(1) == 0)
    def _():
        hist_ref[...] = jnp.zeros_like(hist_ref)
        st_ref[...] = jnp.zeros_like(st_ref)

    x_in = xbc_ref[...]
    hist = hist_ref[...]
    row8 = lax.broadcasted_iota(jnp.int32, (HIST, SSD_XBC), 0)
    conv = cb_ref[...] + cw_ref[SSD_CONV - 1:SSD_CONV, :] * x_in
    for back in range(1, SSD_CONV):
        rolled = pltpu.roll(x_in, back, 0)
        head = jnp.where(row8 < back, pltpu.roll(hist, back, 0), rolled[0:HIST])
        shifted = jnp.concatenate([head, rolled[HIST:]], axis=0)
        conv = conv + cw_ref[SSD_CONV - 1 - back:SSD_CONV - back, :] * shifted
    hist_ref[...] = x_in[TB - HIST:TB, :]
    yield
    xbc = _silu(conv)
    xs = xbc[:, :SSD_W]
    bmat = xbc[:, SSD_W:SSD_W + SSD_GROUPS * SSD_STATE]
    cmat = xbc[:, SSD_W + SSD_GROUPS * SSD_STATE:]
    yield

    dt = _softplus(dt_ref[...] + dtb_ref[...])
    da = dt * -jnp.exp(alog_ref[...])
    acs = _mm_exact_lhs(tril_ref[...], da, 2)
    yield
    dt_rep = jnp.concatenate([jnp.broadcast_to(dt[:, h:h + 1], (TB, LANES)) for h in range(SSD_HEADS)], axis=1)
    acs_rep = jnp.concatenate([jnp.broadcast_to(acs[:, h:h + 1], (TB, LANES)) for h in range(SSD_HEADS)], axis=1)

    lane = lax.broadcasted_iota(jnp.int32, (TB, LANES), 1)
    lo = lane < SSD_P

    def pair64(rep, q):
        return jnp.where(lo, rep[:, (2 * q) * LANES:(2 * q + 1) * LANES],
                         rep[:, (2 * q + 1) * LANES:(2 * q + 2) * LANES])

    dt64 = jnp.concatenate([pair64(dt_rep, q) for q in range(SSD_HEADS // 2)], axis=1)
    acs64 = jnp.concatenate([pair64(acs_rep, q) for q in range(SSD_HEADS // 2)], axis=1)
    xdt = xs * dt64
    last = [acs64[(c + 1) * L - 1:(c + 1) * L, :] for c in range(nc)]
    acs_last = jnp.concatenate([jnp.broadcast_to(t, (L, SSD_W)) for t in last], axis=0)
    xdec = xdt * jnp.exp(acs_last - acs64)
    e_acs = jnp.exp(acs64)
    chunk_decay = [jnp.exp(t) for t in last]
    yield

    li = lax.broadcasted_iota(jnp.int32, (L, L), 0)
    si = lax.broadcasted_iota(jnp.int32, (L, L), 1)
    causal = li >= si
    eye = li == si
    glane = lax.broadcasted_iota(jnp.int32, (L, SSD_GW), 1)
    hpg = SSD_HEADS // SSD_GROUPS
    G = SSD_GROUPS
    inst = [(c, gi) for c in range(nc) for gi in range(G)]
    every = range(len(inst))
    rows = lambda c: slice(c * L, (c + 1) * L)
    gsl = lambda gi: slice(gi * SSD_GW, (gi + 1) * SSD_GW)
    ssl = lambda gi: slice(gi * SSD_STATE, (gi + 1) * SSD_STATE)
    bg = [bmat[rows(c), ssl(gi)] for c, gi in inst]
    cg = [cmat[rows(c), ssl(gi)] for c, gi in inst]
    scores = [_mm_nt(cg[i], bg[i]) for i in every]
    yield
    wcat, xst = [], []
    for i, (c, gi) in enumerate(inst):
        if i == len(inst) // 2:
            yield
        xg = xdt[rows(c), gsl(gi)]
        wm, xm = [], []
        for j in range(hpg):
            hd = gi * hpg + j
            col = acs_rep[rows(c), hd * LANES:(hd + 1) * LANES]
            rowv = jnp.sum(jnp.where(eye, col, 0.0), axis=0, keepdims=True)
            wm.append(jnp.where(causal, scores[i] * jnp.exp(col - rowv), 0.0))
            xm.append(jnp.where((glane // SSD_P) == j, xg, 0.0))
        wcat.append(jnp.concatenate(wm, axis=1))
        xst.append(jnp.concatenate(xm, axis=0))
    yield
    y_diag = [_mm(wcat[i], xst[i]) for i in every]
    upd = [_mm_tn(bg[i], xdec[rows(c), gsl(gi)]) for i, (c, gi) in enumerate(inst)]
    yield
    st = [st_ref[gi] for gi in range(G)]
    y_rows = []
    for c in range(nc):
        y_off = [_mm(cg[c * G + gi], st[gi]) * e_acs[rows(c), gsl(gi)] for gi in range(G)]
        st = [chunk_decay[c][:, gsl(gi)] * st[gi] + upd[c * G + gi] for gi in range(G)]
        y_rows.append(jnp.concatenate([y_diag[c * G + gi] + y_off[gi] for gi in range(G)], axis=1))
    for gi in range(G):
        st_ref[gi] = st[gi]
    yield
    y = jnp.concatenate(y_rows, axis=0) + xs * dskip_ref[...]
    y = y * _silu(z_ref[...])
    outs = []
    for gi in range(SSD_GROUPS):
        yg = y[:, gi * SSD_GW:(gi + 1) * SSD_GW]
        outs.append(yg * lax.rsqrt(jnp.mean(yg * yg, axis=-1, keepdims=True) + SSD_NORM_EPS))
    result.append(jnp.concatenate(outs, axis=1) * ng_ref[...])


def _s5_kernel(u_ref, lre_ref, lim_ref, lstep_ref, wb_ref, cre_ref, cim_ref, d_ref,
               gw_ref, gb_ref, perm_ref, permt_ref, o_ref, sre_ref, sim_ref, bur_ref, bui_ref, par_ref,
               *, nb, tt):
    @pl.when(pl.program_id(0) == 0)
    def _():
        sre_ref[...] = jnp.zeros_like(sre_ref)
        sim_ref[...] = jnp.zeros_like(sim_ref)
        lr = lre_ref[...]
        li = lim_ref[...]
        step = jnp.exp(lstep_ref[...])
        mag = jnp.exp(lr * step)
        a_re = mag * jnp.cos(li * step)
        a_im = mag * jnp.sin(li * step)
        den = lr * lr + li * li
        par_ref[0:1, :] = a_re
        par_ref[1:2, :] = a_im
        par_ref[2:3, :] = ((a_re - 1.0) * lr + a_im * li) / den
        par_ref[3:4, :] = (a_im * lr - (a_re - 1.0) * li) / den

    ab_re = par_ref[0:1, :]
    ab_im = par_ref[1:2, :]
    coef_re = par_ref[2:3, :]
    coef_im = par_ref[3:4, :]

    ub = jnp.dot(perm_ref[...], u_ref[...].reshape(nb * tt, S5_W).astype(bf16),
                 preferred_element_type=f32).astype(bf16)
    u = ub.astype(f32)
    in_per_tile = LANES // S5_STATE * S5_GROUP
    for j in range(S5_SN // LANES):
        us = slice(j * in_per_tile // LANES * LANES, (j * in_per_tile // LANES + 1) * LANES)
        ls = slice(j * LANES, (j + 1) * LANES)
        xri = jnp.dot(ub[:, us], wb_ref[us, 2 * j * LANES:2 * (j + 1) * LANES], preferred_element_type=f32)
        xr = xri[:, :LANES]
        xi = xri[:, LANES:]
        bur_ref[:, ls] = coef_re[:, ls] * xr - coef_im[:, ls] * xi
        bui_ref[:, ls] = coef_re[:, ls] * xi + coef_im[:, ls] * xr

    LB = 4 * LANES
    for jb in range(S5_SN // LB):
        ls = slice(jb * LB, (jb + 1) * LB)
        ar = jnp.broadcast_to(ab_re[:, ls], (nb, LB))
        ai = jnp.broadcast_to(ab_im[:, ls], (nb, LB))

        def body(t, carry):
            s_re, s_im = carry
            rows = pl.ds(pl.multiple_of(t * nb, nb), nb)
            n_re = ar * s_re - ai * s_im + bur_ref[rows, ls]
            n_im = ar * s_im + ai * s_re + bui_ref[rows, ls]
            bur_ref[rows, ls] = n_re
            bui_ref[rows, ls] = n_im
            return n_re, n_im

        s_re, s_im = lax.fori_loop(0, tt, body, (sre_ref[:, ls], sim_ref[:, ls]), unroll=2)
        sre_ref[:, ls] = s_re
        sim_ref[:, ls] = s_im

    st_per_tile = LANES // S5_GROUP * S5_STATE
    ys = []
    for m in range(S5_W // LANES):
        ks = slice(m * st_per_tile, (m + 1) * st_per_tile)
        os_ = slice(m * LANES, (m + 1) * LANES)
        ys.append(jnp.dot(bur_ref[:, ks].astype(bf16), cre_ref[ks, os_], preferred_element_type=f32)
                  - jnp.dot(bui_ref[:, ks].astype(bf16), cim_ref[ks, os_], preferred_element_type=f32))
    y = jnp.concatenate(ys, axis=1) + d_ref[...] * u
    zg = 0.5 * y * (1.0 + jnp.tanh(0.7978845608028654 * (y + 0.044715 * (y * y * y))))
    out = (zg * _sigmoid(_mm(zg, gw_ref[...]) + gb_ref[...])).astype(bf16)
    o_ref[...] = jnp.dot(permt_ref[...], out, preferred_element_type=f32).astype(bf16).reshape(nb, tt, S5_W)


def _s5(u, layer_consts, l, tt):
    B, T, _ = u.shape
    blk = tt * B
    src = jnp.arange(blk)
    perm = (((src % B) * tt + src // B)[:, None] == src[None, :]).astype(bf16)
    shared = (perm, perm.T)
    spec = pl.BlockSpec((B, tt, S5_W), lambda i: (0, i, 0))
    return pl.pallas_call(
        functools.partial(_s5_kernel, nb=B, tt=tt),
        grid=(T // tt,),
        in_specs=[spec] + [_layer_spec(c, l) for c in layer_consts] + [_const_spec(c) for c in shared],
        out_specs=spec,
        out_shape=jax.ShapeDtypeStruct((B, T, S5_W), bf16),
        scratch_shapes=[pltpu.VMEM((B, S5_SN), f32), pltpu.VMEM((B, S5_SN), f32),
                        pltpu.VMEM((blk, S5_SN), f32), pltpu.VMEM((blk, S5_SN), f32),
                        pltpu.VMEM((8, S5_SN), f32)],
        compiler_params=_cparams(("arbitrary",)),
        name="s5",
    )(u, *layer_consts, *shared)


GATE_BLOCKS = N_BRANCHES * D_MODEL // S5_W


N_MERGE_W = 4


def _merge_kernel(x_ref, oa_ref, oc_ref, z_ref, xbc_ref, dt_ref, g_ref, *rest, nc):
    wg_refs = rest[:GATE_BLOCKS]
    pa_ref, pb_ref, pc_ref, wo_ref = rest[GATE_BLOCKS:GATE_BLOCKS + N_MERGE_W]
    ssd_refs = rest[GATE_BLOCKS + N_MERGE_W:-3]
    out_ref, hist_ref, st_ref = rest[-3:]
    per = GATE_BLOCKS // N_BRANCHES
    ssd_out = []
    ssd = _ssd_stages(z_ref, xbc_ref, dt_ref, *ssd_refs, hist_ref, st_ref, ssd_out, nc=nc)
    x = x_ref[...]
    hb = _rms(x, g_ref[...], NORM_EPS).astype(bf16)
    next(ssd)
    logit_blocks = []
    for w in wg_refs:
        logit_blocks.append(jnp.dot(hb, w[...], preferred_element_type=f32))
        next(ssd, None)
    gates = [_sigmoid(jnp.concatenate(logit_blocks[i * per:(i + 1) * per], axis=1)) for i in range(N_BRANCHES)]
    ya = jnp.dot(oa_ref[...].astype(bf16), pa_ref[...], preferred_element_type=f32)
    next(ssd, None)
    yc = jnp.dot(oc_ref[...].astype(bf16), pc_ref[...], preferred_element_type=f32)
    for _ in ssd:
        pass
    merged = gates[0] * ya + gates[2] * yc
    yb = jnp.dot(ssd_out[0].astype(bf16), pb_ref[...], preferred_element_type=f32)
    merged = merged + gates[1] * yb
    out_ref[...] = x + jnp.dot(merged.astype(bf16), wo_ref[...], preferred_element_type=f32)


def _merge(x, oa, oc, z, xbc, dt, g, w_ug, layer_consts, ssd_layer, ssd_shared, l, tm):
    B, T, D = x.shape
    nc = tm // SSD_CHUNK
    row = lambda width: pl.BlockSpec((None, tm, width), lambda b, j: (b, j, 0))
    gate_specs = [_col_block_spec(w_ug, l, S5_W, 1 + k) for k in range(GATE_BLOCKS)]
    return pl.pallas_call(
        functools.partial(_merge_kernel, nc=nc),
        grid=(B, T // tm),
        in_specs=[row(D), row(RW_W), row(S5_W), row(SSD_W), row(SSD_XBC), row(DT_PAD), _layer_spec(g, l)]
                 + gate_specs + [_layer_spec(c, l) for c in layer_consts]
                 + [_layer_spec(c, l) for c in ssd_layer] + [_const_spec(c) for c in ssd_shared],
        out_specs=row(D),
        out_shape=jax.ShapeDtypeStruct((B, T, D), f32),
        scratch_shapes=[pltpu.VMEM((8, SSD_XBC), f32),
                        pltpu.VMEM((SSD_GROUPS, SSD_STATE, SSD_GW), f32)],
        compiler_params=_cparams(("parallel", "arbitrary")),
        name="merge_ssd",
    )(x, oa, oc, z, xbc, dt, g, *([w_ug] * GATE_BLOCKS), *layer_consts, *ssd_layer, *ssd_shared)


def _ffn_kernel(x_ref, g_ref, w1_ref, w2_ref, fg_ref, out_ref, *, final):
    x = x_ref[...]
    hb = _rms(x, g_ref[...], NORM_EPS).astype(bf16)
    gate = jnp.dot(hb, w1_ref[:, :FFN_HIDDEN], preferred_element_type=f32)
    up = jnp.dot(hb, w1_ref[:, FFN_HIDDEN:], preferred_element_type=f32)
    act = (_silu(gate) * up).astype(bf16)
    y = x + jnp.dot(act, w2_ref[...], preferred_element_type=f32)
    if final:
        y = _rms(y, fg_ref[...], NORM_EPS)
    out_ref[...] = y


def _ffn(x, layer_consts, fg, l, tm, final):
    B, T, D = x.shape
    row = pl.BlockSpec((None, tm, D), lambda b, j: (b, j, 0))
    return pl.pallas_call(
        functools.partial(_ffn_kernel, final=final),
        grid=(B, T // tm),
        in_specs=[row] + [_layer_spec(c, l) for c in layer_consts] + [_const_spec(fg)],
        out_specs=row,
        out_shape=jax.ShapeDtypeStruct((B, T, D), f32),
        compiler_params=_cparams(("parallel", "parallel")),
        name="ffn",
    )(x, *layer_consts, fg)


W_IN_A = RW_COLS + SSD_W + SSD_XBC
W_IN_UG = W_IN_A + SSD_HEADS
W_IN_COLS = W_IN_UG + S5_W + N_BRANCHES * D_MODEL


def _win_split_kernel(wt_ref, wa_ref, wug_ref, wdt_ref):
    wt = wt_ref[...]
    wa_ref[...] = wt[:W_IN_A, :].T.astype(bf16)
    wug_ref[...] = wt[W_IN_UG:, :].T.astype(bf16)
    dt_rows = jnp.concatenate([wt[W_IN_A:W_IN_UG, :],
                               jnp.zeros((DT_PAD - SSD_HEADS, wt.shape[1]), f32)], axis=0)
    wdt_ref[...] = dt_rows.T.astype(bf16)


def _win_split(w_in, rb):
    depth, D, cols = w_in.shape
    w_t = jnp.swapaxes(w_in, 1, 2)
    spec = lambda width: pl.BlockSpec((None, rb, width), lambda l, i: (l, i, 0))
    return pl.pallas_call(
        _win_split_kernel,
        grid=(depth, D // rb),
        in_specs=[pl.BlockSpec((None, cols, rb), lambda l, i: (l, 0, i))],
        out_specs=[spec(W_IN_A), spec(cols - W_IN_UG), spec(DT_PAD)],
        out_shape=[jax.ShapeDtypeStruct((depth, D, W_IN_A), bf16),
                   jax.ShapeDtypeStruct((depth, D, cols - W_IN_UG), bf16),
                   jax.ShapeDtypeStruct((depth, D, DT_PAD), bf16)],
        compiler_params=_cparams(("parallel", "parallel")),
        name="w_in_split",
    )(w_t)


def _rows(v):
    return v.reshape(v.shape[0], 1, -1).astype(f32)


def _s5_block_diag(w2d, row_group, col_group):
    tiled = jnp.tile(w2d, (1, 1, S5_GROUPS))
    r = jnp.arange(tiled.shape[1]) // row_group
    c = jnp.arange(tiled.shape[2]) // col_group
    return jnp.where((r[:, None] == c[None, :])[None], tiled, 0.0).astype(bf16)


def _tile_tm(T):
    for tm in (512, 256, 128):
        if T % tm == 0:
            return tm
    raise ValueError("sequence length must be a multiple of 128")


def kernel(x, norm_mix, w_in, rwkv_mu, rwkv_w0, rwkv_w_up, rwkv_a0, rwkv_a_up, rwkv_g_up, rwkv_k_k, rwkv_k_a, rwkv_r_k, rwkv_ln_g, rwkv_ln_b, proj_a, ssd_conv_w, ssd_conv_b, ssd_dt_bias, ssd_a_log, ssd_d, ssd_norm_g, proj_b, s5_lam_re, s5_lam_im, s5_log_step, s5_b_re, s5_b_im, s5_c_re, s5_c_im, s5_d, s5_glu_w, s5_glu_b, proj_c, w_out, norm_ffn, ffn_w_in, ffn_w_out, final_norm):
    B, T, D = x.shape
    depth = w_in.shape[0]
    rw_nc = RW_CHUNKS_PER_STEP
    assert D == D_MODEL and T % SSD_CHUNK == 0 and B % 8 == 0 and T % (rw_nc * RW_CHUNK) == 0
    tm = _tile_tm(T)
    hid = jnp.arange(2 * RW_N) // RW_N
    head_ones = (hid[:, None] == hid[None, :]).astype(bf16)
    c_idx = jnp.arange(rw_nc * RW_CHUNK)
    tril_rw = ((c_idx[:, None] >= c_idx[None, :])
               & (c_idx[:, None] // RW_CHUNK == c_idx[None, :] // RW_CHUNK)).astype(bf16)
    l_idx = jnp.arange(tm)
    tril_ssd = ((l_idx[:, None] >= l_idx[None, :])
                & (l_idx[:, None] // SSD_CHUNK == l_idx[None, :] // SSD_CHUNK)).astype(bf16)

    g_mix = _rows(norm_mix)
    assert w_in.shape[2] == W_IN_COLS
    w_a, w_ug, w_dt = _win_split(w_in, 256)
    zr = jnp.zeros((depth, RW_DECAY_RANK, RW_W), f32)
    w_wa = jnp.concatenate([jnp.concatenate([rwkv_w_up, zr], axis=2),
                            jnp.concatenate([zr, rwkv_a_up], axis=2)], axis=1).astype(bf16)
    rw_layer = (_rows(rwkv_mu), jnp.concatenate([_rows(rwkv_w0), _rows(rwkv_a0)], axis=2), w_wa,
                rwkv_g_up.astype(bf16), _rows(rwkv_k_k), _rows(rwkv_k_a), _rows(rwkv_r_k),
                _rows(rwkv_ln_g), _rows(rwkv_ln_b))
    ssd_layer = (
        jnp.concatenate([ssd_conv_w, jnp.zeros((depth, 8 - SSD_CONV, SSD_XBC), f32)], axis=1),
        _rows(ssd_conv_b),
        jnp.concatenate([_rows(ssd_dt_bias), jnp.zeros((depth, 1, DT_PAD - SSD_HEADS), f32)], axis=2),
        jnp.concatenate([_rows(ssd_a_log), jnp.zeros((depth, 1, DT_PAD - SSD_HEADS), f32)], axis=2),
        _rows(jnp.repeat(ssd_d, SSD_P, axis=1)),
        _rows(ssd_norm_g))
    b_rows = lambda b: jnp.swapaxes(b, 2, 3).reshape(depth, S5_W, S5_STATE)
    c_rows = lambda c: jnp.swapaxes(c, 2, 3).reshape(depth, S5_SN, S5_GROUP)
    nlt = S5_SN // LANES
    w_b = jnp.stack([_s5_block_diag(b_rows(s5_b_re), S5_GROUP, S5_STATE).reshape(depth, S5_W, nlt, LANES),
                     _s5_block_diag(b_rows(s5_b_im), S5_GROUP, S5_STATE).reshape(depth, S5_W, nlt, LANES)],
                    axis=3).reshape(depth, S5_W, 2 * S5_SN)
    s5_layer = (
        _rows(s5_lam_re), _rows(s5_lam_im), _rows(jnp.repeat(s5_log_step, S5_STATE, axis=1)),
        w_b,
        _s5_block_diag(c_rows(s5_c_re), S5_STATE, S5_GROUP),
        _s5_block_diag(c_rows(s5_c_im), S5_STATE, S5_GROUP),
        _rows(s5_d), s5_glu_w.astype(bf16), _rows(s5_glu_b))
    merge_layer = (proj_a.astype(bf16), proj_b.astype(bf16), proj_c.astype(bf16), w_out.astype(bf16))
    ffn_layer = (_rows(norm_ffn), ffn_w_in.astype(bf16), ffn_w_out.astype(bf16))
    fg = final_norm.reshape(1, D).astype(f32)
    s5_tt = 32 if T % 32 == 0 else 8

    for l in range(depth):
        p, z, xbc, u, dt = _inproj(x, g_mix, w_a, w_ug, w_dt, l, tm)
        oa = _rwkv(p, rw_layer, (head_ones, tril_rw), l, rw_nc)
        oc = _s5(u, s5_layer, l, s5_tt)
        x = _merge(x, oa, oc, z, xbc, dt, g_mix, w_ug, merge_layer, ssd_layer, (tril_ssd,), l, tm)
        x = _ffn(x, ffn_layer, fg, l, tm, final=(l == depth - 1))
    return x
```

```python
import functools

import jax
import jax.numpy as jnp
from jax import lax
from jax.experimental import pallas as pl
from jax.experimental.pallas import tpu as pltpu

f32 = jnp.float32
bf16 = jnp.bfloat16

D_MODEL = 1024
RW_HEADS = 8
RW_N = 64
RW_W = RW_HEADS * RW_N
RW_DECAY_RANK = 64
RW_ICLR_RANK = 64
RW_GATE_RANK = 128
RW_COLS = 3 * RW_W + RW_DECAY_RANK + RW_ICLR_RANK + RW_GATE_RANK
RW_LN_EPS = 64e-5
RW_CHUNK = 64
RW_CHUNKS_PER_STEP = 4
SSD_HEADS = 8
SSD_P = 64
SSD_W = SSD_HEADS * SSD_P
SSD_GROUPS = 2
SSD_STATE = 128
SSD_CONV = 4
SSD_CHUNK = 128
SSD_XBC = SSD_W + 2 * SSD_GROUPS * SSD_STATE
SSD_NORM_EPS = 1e-5
SSD_GW = SSD_W // SSD_GROUPS
S5_GROUP = 16
S5_GROUPS = 32
S5_W = S5_GROUP * S5_GROUPS
S5_STATE = 64
S5_SN = S5_GROUPS * S5_STATE
N_BRANCHES = 3
FFN_HIDDEN = 2816
NORM_EPS = 1e-6
LANES = 128
DT_PAD = LANES

VMEM_LIMIT = 56 * 1024 * 1024


def _cparams(sem):
    return pltpu.CompilerParams(dimension_semantics=sem, vmem_limit_bytes=VMEM_LIMIT)


def _const_spec(arr):
    nd = arr.ndim
    return pl.BlockSpec(arr.shape, lambda *_: (0,) * nd, pipeline_mode=pl.Buffered(1))


def _layer_spec(arr, l):
    nd = arr.ndim - 1
    return pl.BlockSpec((None,) + arr.shape[1:], lambda *_: (l,) + (0,) * nd,
                        pipeline_mode=pl.Buffered(1))


def _mm(a, b):
    return jnp.dot(a.astype(bf16), b.astype(bf16), preferred_element_type=f32)


def _mm_nt(a, b):
    return lax.dot_general(a.astype(bf16), b.astype(bf16), (((1,), (1,)), ((), ())),
                           preferred_element_type=f32)


def _mm_tn(a, b):
    return lax.dot_general(a.astype(bf16), b.astype(bf16), (((0,), (0,)), ((), ())),
                           preferred_element_type=f32)


def _split(x, parts):
    out = []
    for _ in range(parts - 1):
        hi = x.astype(bf16)
        out.append(hi)
        x = x - hi.astype(f32)
    out.append(x.astype(bf16))
    return out


def _mm_exact_rhs(a, b_bf16, parts):
    acc = None
    for piece in _split(a, parts):
        t = jnp.dot(piece, b_bf16, preferred_element_type=f32)
        acc = t if acc is None else acc + t
    return acc


def _mm_exact_lhs(a_bf16, b, parts):
    acc = None
    for piece in _split(b, parts):
        t = jnp.dot(a_bf16, piece, preferred_element_type=f32)
        acc = t if acc is None else acc + t
    return acc


def _rms(x, g, eps):
    return x * lax.rsqrt(jnp.mean(x * x, axis=-1, keepdims=True) + eps) * g


def _sigmoid(x):
    return 1.0 / (1.0 + jnp.exp(-x))


def _softplus(x):
    return jnp.maximum(x, 0.0) + jnp.log(1.0 + jnp.exp(-jnp.abs(x)))


def _silu(x):
    return x * _sigmoid(x)


def _inproj_kernel(x_ref, g_ref, wa_ref, wu_ref, wdt_ref, p_ref, z_ref, xbc_ref, u_ref, dt_ref):
    hb = _rms(x_ref[...], g_ref[...], NORM_EPS).astype(bf16)
    o = 0
    for ref, width in ((p_ref, RW_COLS), (z_ref, SSD_W), (xbc_ref, SSD_XBC)):
        ref[...] = jnp.dot(hb, wa_ref[:, o:o + width], preferred_element_type=f32)
        o += width
    u_ref[...] = jnp.dot(hb, wu_ref[...], preferred_element_type=f32)
    dt_ref[...] = jnp.dot(hb, wdt_ref[...], preferred_element_type=f32)


def _col_block_spec(arr, l, width, idx):
    return pl.BlockSpec((None, arr.shape[1], width), lambda *_: (l, 0, idx), pipeline_mode=pl.Buffered(1))


def _inproj(x, g, w_a, w_ug, w_dt, l, tm):
    B, T, D = x.shape
    nt = T // tm
    row = lambda width: pl.BlockSpec((None, tm, width), lambda b, j: (b, j, 0))
    return pl.pallas_call(
        _inproj_kernel,
        grid=(B, nt),
        in_specs=[row(D), _layer_spec(g, l), _layer_spec(w_a, l), _col_block_spec(w_ug, l, S5_W, 0),
                  _layer_spec(w_dt, l)],
        out_specs=[row(RW_COLS), row(SSD_W), row(SSD_XBC), row(S5_W), row(DT_PAD)],
        out_shape=[jax.ShapeDtypeStruct((B, T, RW_COLS), f32),
                   jax.ShapeDtypeStruct((B, T, SSD_W), f32),
                   jax.ShapeDtypeStruct((B, T, SSD_XBC), f32),
                   jax.ShapeDtypeStruct((B, T, S5_W), f32),
                   jax.ShapeDtypeStruct((B, T, DT_PAD), f32)],
        compiler_params=_cparams(("parallel", "parallel")),
        name="inproj",
    )(x, g, w_a, w_ug, w_dt)


def _stack_heads(x):
    lane = lax.broadcasted_iota(jnp.int32, x.shape, 1)
    lo = lane < RW_N
    return jnp.concatenate([jnp.where(lo, x, 0.0), jnp.where(lo, 0.0, x)], axis=0)


def _head_sums(x, ones_pair):
    R = x.shape[0]
    PW = 2 * RW_N
    st = jnp.concatenate([x[:, q * PW:(q + 1) * PW] for q in range(RW_W // PW)], axis=0)
    s = _mm_exact_rhs(st, ones_pair, 2)
    return jnp.concatenate([s[q * R:(q + 1) * R] for q in range(RW_W // PW)], axis=1)


def _rwkv_kernel(p_ref, mu_ref, w0a0_ref, wwa_ref, gup_ref, kk_ref, ka_ref, rk_ref, lng_ref, lnb_ref,
                 ones_ref, tril_ref, o_ref, prev_ref, h_ref, *, nc):
    C = RW_CHUNK
    TB = nc * C
    PW = 2 * RW_N
    NP = RW_HEADS // 2

    @pl.when(pl.program_id(1) == 0)
    def _():
        prev_ref[...] = jnp.zeros_like(prev_ref)
        h_ref[...] = jnp.zeros_like(h_ref)

    p = p_ref[...]
    row = lax.broadcasted_iota(jnp.int32, p.shape, 0)
    shifted = jnp.where(row == 0, prev_ref[...], pltpu.roll(p, 1, 0))
    prev_ref[...] = p[TB - 1:TB, :]
    pm = p + (shifted - p) * mu_ref[...]
    r = pm[:, 0:RW_W]
    k = pm[:, RW_W:2 * RW_W]
    v = pm[:, 2 * RW_W:3 * RW_W]
    wa_in = pm[:, 3 * RW_W:3 * RW_W + PW]
    gd = pm[:, 3 * RW_W + PW:RW_COLS]
    lane = lax.broadcasted_iota(jnp.int32, wa_in.shape, 1)
    wa_in = jnp.where(lane < RW_DECAY_RANK, jnp.tanh(wa_in), wa_in)
    wa = w0a0_ref[...] + _mm(wa_in, wwa_ref[...])
    w_log = -_softplus(-wa[:, :RW_W]) - 0.5
    lw = -jnp.exp(w_log)
    a = _sigmoid(wa[:, RW_W:])
    g = _mm(_sigmoid(gd), gup_ref[...])

    ones = ones_ref[...]
    kk = k * kk_ref[...]
    kk = kk * lax.rsqrt(jnp.maximum(_head_sums(kk * kk, ones), 1e-24))
    k2 = k * (1.0 + (a - 1.0) * ka_ref[...])

    lc = _mm_exact_lhs(tril_ref[...], lw, 2)
    lc_last = jnp.concatenate(
        [jnp.broadcast_to(lc[(c + 1) * C - 1:(c + 1) * C, :], (C, RW_W)) for c in range(nc)], axis=0)
    e_pos = jnp.exp(lc)
    e_neg = jnp.exp(-lc)
    e_end = jnp.exp(lc_last - lc)
    kka = kk * a
    r_hat = r * e_pos
    a_hat = -kk * jnp.exp(lc - lw)
    b_hat = kka * e_neg
    k_hat = k2 * e_neg
    b_til = kka * e_end
    k_til = k2 * e_end

    ri = lax.broadcasted_iota(jnp.int32, (4 * C, 4 * C), 0)
    ci = lax.broadcasted_iota(jnp.int32, (4 * C, 4 * C), 1)
    tri = (ri % C) + jnp.where(ri < 2 * C, 0, 1) > (ci % C)
    ei = lax.broadcasted_iota(jnp.int32, (PW, PW), 0)
    ej = lax.broadcasted_iota(jnp.int32, (PW, PW), 1)
    eye = ei == ej
    blk = (ei // RW_N) == (ej // RW_N)
    eye_f = jnp.where(eye, 1.0, 0.0)

    inst = [(c, q) for c in range(nc) for q in range(NP)]
    every = range(len(inst))

    def stacked(t):
        return [_stack_heads(t[c * C:(c + 1) * C, q * PW:(q + 1) * PW]) for c, q in inst]

    am, rm, bm, km = stacked(a_hat), stacked(r_hat), stacked(b_hat), stacked(k_hat)
    btm, ktm, vm = stacked(b_til), stacked(k_til), stacked(v)
    a_all = [_mm_nt(jnp.concatenate([am[i], rm[i]], axis=0), jnp.concatenate([bm[i], km[i]], axis=0))
             for i in every]
    a_all = [jnp.where(tri, t, 0.0) for t in a_all]
    pw = [t[:2 * C, :2 * C] for t in a_all]
    t_inv = [eye_f + t for t in pw]
    pw = [_mm(t, t) for t in pw]
    for _ in range(4):
        both = [_mm(jnp.concatenate([pw[i], t_inv[i]], axis=0), pw[i]) for i in every]
        t_inv = [t_inv[i] + both[i][2 * C:] for i in every]
        pw = [t[:2 * C] for t in both]
    t_inv = [t_inv[i] + _mm(t_inv[i], pw[i]) for i in every]
    av = [_mm(a_all[i][:, 2 * C:], vm[i]) for i in every]
    pq = [_mm(t_inv[i], jnp.concatenate([am[i], av[i][:2 * C]], axis=1)) for i in every]
    xy = [_mm(a_all[i][2 * C:, :2 * C], pq[i]) for i in every]
    mk = [_mm_tn(btm[i], pq[i][:, :PW]) for i in every]
    gk = [_mm_tn(jnp.concatenate([btm[i], ktm[i]], axis=0), jnp.concatenate([pq[i][:, PW:], vm[i]], axis=0))
          for i in every]
    mx = [jnp.concatenate([mk[i], rm[i] + xy[i][:, :PW]], axis=0) for i in every]
    ym = [xy[i][:, PW:] + av[i][2 * C:] for i in every]
    gam_col = [jnp.sum(jnp.where(eye, jnp.exp(lc[(c + 1) * C - 1:(c + 1) * C, q * PW:(q + 1) * PW]), 0.0),
                       axis=1, keepdims=True) for c, q in inst]

    hs = [h_ref[q] for q in range(NP)]
    o_rows = []
    for c in range(nc):
        seq = [_mm(mx[c * NP + q], hs[q]) for q in range(NP)]
        hs = [jnp.where(blk, gam_col[c * NP + q] * hs[q] + seq[q][:PW] + gk[c * NP + q], 0.0) for q in range(NP)]
        om = [seq[q][PW:] + ym[c * NP + q] for q in range(NP)]
        o_rows.append(jnp.concatenate([t[:C] + t[C:] for t in om], axis=1))
    for q in range(NP):
        h_ref[q] = hs[q]
    o = jnp.concatenate(o_rows, axis=0)

    inv_n = 1.0 / RW_N
    mean = _head_sums(o, ones) * inv_n
    oc = o - mean
    var = _head_sums(oc * oc, ones) * inv_n
    o = oc * lax.rsqrt(var + RW_LN_EPS) * lng_ref[...] + lnb_ref[...]
    bonus = _head_sums(r * k2 * rk_ref[...], ones) * v
    o_ref[...] = (o + bonus) * g


def _rwkv(p, layer_consts, shared_consts, l, nc):
    B, T, _ = p.shape
    TB = nc * RW_CHUNK
    return pl.pallas_call(
        functools.partial(_rwkv_kernel, nc=nc),
        grid=(B, T // TB),
        in_specs=[pl.BlockSpec((None, TB, RW_COLS), lambda b, j: (b, j, 0))]
                 + [_layer_spec(c, l) for c in layer_consts] + [_const_spec(c) for c in shared_consts],
        out_specs=pl.BlockSpec((None, TB, RW_W), lambda b, j: (b, j, 0)),
        out_shape=jax.ShapeDtypeStruct((B, T, RW_W), f32),
        scratch_shapes=[pltpu.VMEM((1, RW_COLS), f32),
                        pltpu.VMEM((RW_HEADS // 2, 2 * RW_N, 2 * RW_N), f32)],
        compiler_params=_cparams(("parallel", "arbitrary")),
        name="rwkv7",
    )(p, *layer_consts, *shared_consts)


def _ssd_stages(z_ref, xbc_ref, dt_ref, cw_ref, cb_ref, dtb_ref, alog_ref, dskip_ref, ng_ref,
                tril_ref, hist_ref, st_ref, result, *, nc):
    L = SSD_CHUNK
    TB = nc * L
    HIST = 8

    @pl.when(pl.program_id(1) == 0)
    def _():
        hist_ref[...] = jnp.zeros_like(hist_ref)
        st_ref[...] = jnp.zeros_like(st_ref)

    x_in = xbc_ref[...]
    hist = hist_ref[...]
    row8 = lax.broadcasted_iota(jnp.int32, (HIST, SSD_XBC), 0)
    conv = cb_ref[...] + cw_ref[SSD_CONV - 1:SSD_CONV, :] * x_in
    for back in range(1, SSD_CONV):
        rolled = pltpu.roll(x_in, back, 0)
        head = jnp.where(row8 < back, pltpu.roll(hist, back, 0), rolled[0:HIST])
        shifted = jnp.concatenate([head, rolled[HIST:]], axis=0)
        conv = conv + cw_ref[SSD_CONV - 1 - back:SSD_CONV - back, :] * shifted
    hist_ref[...] = x_in[TB - HIST:TB, :]
    yield
    xbc = _silu(conv)
    xs = xbc[:, :SSD_W]
    bmat = xbc[:, SSD_W:SSD_W + SSD_GROUPS * SSD_STATE]
    cmat = xbc[:, SSD_W + SSD_GROUPS * SSD_STATE:]
    yield

    dt = _softplus(dt_ref[...] + dtb_ref[...])
    da = dt * -jnp.exp(alog_ref[...])
    acs = _mm_exact_lhs(tril_ref[...], da, 2)
    yield
    dt_rep = jnp.concatenate([jnp.broadcast_to(dt[:, h:h + 1], (TB, LANES)) for h in range(SSD_HEADS)], axis=1)
    acs_rep = jnp.concatenate([jnp.broadcast_to(acs[:, h:h + 1], (TB, LANES)) for h in range(SSD_HEADS)], axis=1)

    lane = lax.broadcasted_iota(jnp.int32, (TB, LANES), 1)
    lo = lane < SSD_P

    def pair64(rep, q):
        return jnp.where(lo, rep[:, (2 * q) * LANES:(2 * q + 1) * LANES],
                         rep[:, (2 * q + 1) * LANES:(2 * q + 2) * LANES])

    dt64 = jnp.concatenate([pair64(dt_rep, q) for q in range(SSD_HEADS // 2)], axis=1)
    acs64 = jnp.concatenate([pair64(acs_rep, q) for q in range(SSD_HEADS // 2)], axis=1)
    xdt = xs * dt64
    last = [acs64[(c + 1) * L - 1:(c + 1) * L, :] for c in range(nc)]
    acs_last = jnp.concatenate([jnp.broadcast_to(t, (L, SSD_W)) for t in last], axis=0)
    xdec = xdt * jnp.exp(acs_last - acs64)
    e_acs = jnp.exp(acs64)
    chunk_decay = [jnp.exp(t) for t in last]
    yield

    li = lax.broadcasted_iota(jnp.int32, (L, L), 0)
    si = lax.broadcasted_iota(jnp.int32, (L, L), 1)
    causal = li >= si
    eye = li == si
    glane = lax.broadcasted_iota(jnp.int32, (L, SSD_GW), 1)
    hpg = SSD_HEADS // SSD_GROUPS
    G = SSD_GROUPS
    inst = [(c, gi) for c in range(nc) for gi in range(G)]
    every = range(len(inst))
    rows = lambda c: slice(c * L, (c + 1) * L)
    gsl = lambda gi: slice(gi * SSD_GW, (gi + 1) * SSD_GW)
    ssl = lambda gi: slice(gi * SSD_STATE, (gi + 1) * SSD_STATE)
    bg = [bmat[rows(c), ssl(gi)] for c, gi in inst]
    cg = [cmat[rows(c), ssl(gi)] for c, gi in inst]
    scores = [_mm_nt(cg[i], bg[i]) for i in every]
    yield
    wcat, xst = [], []
    for i, (c, gi) in enumerate(inst):
        if i == len(inst) // 2:
            yield
        xg = xdt[rows(c), gsl(gi)]
        wm, xm = [], []
        for j in range(hpg):
            hd = gi * hpg + j
            col = acs_rep[rows(c), hd * LANES:(hd + 1) * LANES]
            rowv = jnp.sum(jnp.where(eye, col, 0.0), axis=0, keepdims=True)
            wm.append(jnp.where(causal, scores[i] * jnp.exp(col - rowv), 0.0))
            xm.append(jnp.where((glane // SSD_P) == j, xg, 0.0))
        wcat.append(jnp.concatenate(wm, axis=1))
        xst.append(jnp.concatenate(xm, axis=0))
    yield
    y_diag = [_mm(wcat[i], xst[i]) for i in every]
    upd = [_mm_tn(bg[i], xdec[rows(c), gsl(gi)]) for i, (c, gi) in enumerate(inst)]
    yield
    st = [st_ref[gi] for gi in range(G)]
    y_rows = []
    for c in range(nc):
        y_off = [_mm(cg[c * G + gi], st[gi]) * e_acs[rows(c), gsl(gi)] for gi in range(G)]
        st = [chunk_decay[c][:, gsl(gi)] * st[gi] + upd[c * G + gi] for gi in range(G)]
        y_rows.append(jnp.concatenate([y_diag[c * G + gi] + y_off[gi] for gi in range(G)], axis=1))
    for gi in range(G):
        st_ref[gi] = st[gi]
    yield
    y = jnp.concatenate(y_rows, axis=0) + xs * dskip_ref[...]
    y = y * _silu(z_ref[...])
    outs = []
    for gi in range(SSD_GROUPS):
        yg = y[:, gi * SSD_GW:(gi + 1) * SSD_GW]
        outs.append(yg * lax.rsqrt(jnp.mean(yg * yg, axis=-1, keepdims=True) + SSD_NORM_EPS))
    result.append(jnp.concatenate(outs, axis=1) * ng_ref[...])


def _s5_kernel(u_ref, lre_ref, lim_ref, lstep_ref, wb_ref, cre_ref, cim_ref, d_ref,
               gw_ref, gb_ref, perm_ref, permt_ref, o_ref, sre_ref, sim_ref, bur_ref, bui_ref, par_ref,
               cpr_ref, cpi_ref, *, nb, tt):
    @pl.when(pl.program_id(0) == 0)
    def _():
        sre_ref[...] = jnp.zeros_like(sre_ref)
        sim_ref[...] = jnp.zeros_like(sim_ref)
        lr = lre_ref[...]
        li = lim_ref[...]
        step = jnp.exp(lstep_ref[...])
        mag = jnp.exp(lr * step)
        a_re = mag * jnp.cos(li * step)
        a_im = mag * jnp.sin(li * step)
        den = lr * lr + li * li
        coef_re = ((a_re - 1.0) * lr + a_im * li) / den
        coef_im = (a_im * lr - (a_re - 1.0) * li) / den
        par_ref[0:1, :] = a_re
        par_ref[1:2, :] = a_im
        ei = lax.broadcasted_iota(jnp.int32, (LANES, LANES), 0)
        ej = lax.broadcasted_iota(jnp.int32, (LANES, LANES), 1)
        eye = ei == ej
        for j in range(S5_SN // LANES):
            ls = slice(j * LANES, (j + 1) * LANES)
            kr = jnp.sum(jnp.where(eye, coef_re[:, ls], 0.0), axis=1, keepdims=True)
            ki = jnp.sum(jnp.where(eye, coef_im[:, ls], 0.0), axis=1, keepdims=True)
            c_re = cre_ref[ls, :]
            c_im = cim_ref[ls, :]
            cpr_ref[ls, :] = (c_re * kr - c_im * ki).astype(bf16)
            cpi_ref[ls, :] = (c_re * ki + c_im * kr).astype(bf16)

    ab_re = par_ref[0:1, :]
    ab_im = par_ref[1:2, :]

    ub = jnp.dot(perm_ref[...], u_ref[...].reshape(nb * tt, S5_W).astype(bf16),
                 preferred_element_type=f32).astype(bf16)
    u = ub.astype(f32)
    LB = LANES // S5_GROUP * S5_STATE
    NBLK = S5_SN // LB
    in_per_tile = LANES // S5_STATE * S5_GROUP
    ys = [None] * NBLK

    def project_in(jb):
        for j in range(jb * LB // LANES, (jb + 1) * LB // LANES):
            us = slice(j * in_per_tile // LANES * LANES, (j * in_per_tile // LANES + 1) * LANES)
            ls = slice(j * LANES, (j + 1) * LANES)
            xri = jnp.dot(ub[:, us], wb_ref[us, 2 * j * LANES:2 * (j + 1) * LANES], preferred_element_type=f32)
            bur_ref[:, ls] = xri[:, :LANES]
            bui_ref[:, ls] = xri[:, LANES:]
            yield

    def project_out(m):
        ks = slice(m * LB, (m + 1) * LB)
        os_ = slice(m * LANES, (m + 1) * LANES)
        y_re = jnp.dot(bur_ref[:, ks].astype(bf16), cpr_ref[ks, os_], preferred_element_type=f32)
        yield
        ys[m] = y_re - jnp.dot(bui_ref[:, ks].astype(bf16), cpi_ref[ks, os_], preferred_element_type=f32)
        yield

    def chain(*gens):
        for gen in gens:
            yield from gen

    for _ in project_in(0):
        pass
    for jb in range(NBLK):
        side = chain(project_in(jb + 1) if jb + 1 < NBLK else (), project_out(jb - 1) if jb > 0 else ())
        ls = slice(jb * LB, (jb + 1) * LB)
        ar = jnp.broadcast_to(ab_re[:, ls], (nb, LB))
        ai = jnp.broadcast_to(ab_im[:, ls], (nb, LB))
        s_re = sre_ref[:, ls]
        s_im = sim_ref[:, ls]
        for t in range(tt):
            rows = slice(t * nb, (t + 1) * nb)
            s_re, s_im = (ar * s_re - ai * s_im + bur_ref[rows, ls], ar * s_im + ai * s_re + bui_ref[rows, ls])
            bur_ref[rows, ls] = s_re
            bui_ref[rows, ls] = s_im
            if t % 5 == 4:
                next(side, None)
        sre_ref[:, ls] = s_re
        sim_ref[:, ls] = s_im
        for _ in side:
            pass
    for _ in project_out(NBLK - 1):
        pass
    y = jnp.concatenate(ys, axis=1) + d_ref[...] * u
    zg = 0.5 * y * (1.0 + jnp.tanh(0.7978845608028654 * (y + 0.044715 * (y * y * y))))
    out = (zg * _sigmoid(_mm(zg, gw_ref[...]) + gb_ref[...])).astype(bf16)
    o_ref[...] = jnp.dot(permt_ref[...], out, preferred_element_type=f32).astype(bf16).reshape(nb, tt, S5_W)


def _s5(u, layer_consts, l, tt):
    B, T, _ = u.shape
    blk = tt * B
    src = jnp.arange(blk)
    perm = (((src % B) * tt + src // B)[:, None] == src[None, :]).astype(bf16)
    shared = (perm, perm.T)
    spec = pl.BlockSpec((B, tt, S5_W), lambda i: (0, i, 0))
    return pl.pallas_call(
        functools.partial(_s5_kernel, nb=B, tt=tt),
        grid=(T // tt,),
        in_specs=[spec] + [_layer_spec(c, l) for c in layer_consts] + [_const_spec(c) for c in shared],
        out_specs=spec,
        out_shape=jax.ShapeDtypeStruct((B, T, S5_W), bf16),
        scratch_shapes=[pltpu.VMEM((B, S5_SN), f32), pltpu.VMEM((B, S5_SN), f32),
                        pltpu.VMEM((blk, S5_SN), f32), pltpu.VMEM((blk, S5_SN), f32),
                        pltpu.VMEM((8, S5_SN), f32),
                        pltpu.VMEM((S5_SN, S5_W), bf16), pltpu.VMEM((S5_SN, S5_W), bf16)],
        compiler_params=_cparams(("arbitrary",)),
        name="s5",
    )(u, *layer_consts, *shared)


GATE_BLOCKS = N_BRANCHES * D_MODEL // S5_W


N_MERGE_W = 4


def _merge_kernel(x_ref, oa_ref, oc_ref, z_ref, xbc_ref, dt_ref, g_ref, *rest, nc):
    wg_refs = rest[:GATE_BLOCKS]
    pa_ref, pb_ref, pc_ref, wo_ref = rest[GATE_BLOCKS:GATE_BLOCKS + N_MERGE_W]
    ssd_refs = rest[GATE_BLOCKS + N_MERGE_W:-3]
    out_ref, hist_ref, st_ref = rest[-3:]
    per = GATE_BLOCKS // N_BRANCHES
    ssd_out = []
    ssd = _ssd_stages(z_ref, xbc_ref, dt_ref, *ssd_refs, hist_ref, st_ref, ssd_out, nc=nc)
    ya = jnp.dot(oa_ref[...].astype(bf16), pa_ref[...], preferred_element_type=f32)
    next(ssd)
    x = x_ref[...]
    hb = _rms(x, g_ref[...], NORM_EPS).astype(bf16)
    yc = jnp.dot(oc_ref[...].astype(bf16), pc_ref[...], preferred_element_type=f32)
    next(ssd, None)
    logit_blocks = []
    for w in wg_refs:
        logit_blocks.append(jnp.dot(hb, w[...], preferred_element_type=f32))
        next(ssd, None)
    gates = [_sigmoid(jnp.concatenate(logit_blocks[i * per:(i + 1) * per], axis=1)) for i in range(N_BRANCHES)]
    for _ in ssd:
        pass
    merged = gates[0] * ya + gates[2] * yc
    yb = jnp.dot(ssd_out[0].astype(bf16), pb_ref[...], preferred_element_type=f32)
    merged = merged + gates[1] * yb
    out_ref[...] = x + jnp.dot(merged.astype(bf16), wo_ref[...], preferred_element_type=f32)


def _merge(x, oa, oc, z, xbc, dt, g, w_ug, layer_consts, ssd_layer, ssd_shared, l, tm):
    B, T, D = x.shape
    nc = tm // SSD_CHUNK
    row = lambda width: pl.BlockSpec((None, tm, width), lambda b, j: (b, j, 0))
    gate_specs = [_col_block_spec(w_ug, l, S5_W, 1 + k) for k in range(GATE_BLOCKS)]
    return pl.pallas_call(
        functools.partial(_merge_kernel, nc=nc),
        grid=(B, T // tm),
        in_specs=[row(D), row(RW_W), row(S5_W), row(SSD_W), row(SSD_XBC), row(DT_PAD), _layer_spec(g, l)]
                 + gate_specs + [_layer_spec(c, l) for c in layer_consts]
                 + [_layer_spec(c, l) for c in ssd_layer] + [_const_spec(c) for c in ssd_shared],
        out_specs=row(D),
        out_shape=jax.ShapeDtypeStruct((B, T, D), f32),
        scratch_shapes=[pltpu.VMEM((8, SSD_XBC), f32),
                        pltpu.VMEM((SSD_GROUPS, SSD_STATE, SSD_GW), f32)],
        compiler_params=_cparams(("parallel", "arbitrary")),
        name="merge_ssd",
    )(x, oa, oc, z, xbc, dt, g, *([w_ug] * GATE_BLOCKS), *layer_consts, *ssd_layer, *ssd_shared)


def _ffn_kernel(x_ref, g_ref, w1_ref, w2_ref, fg_ref, out_ref, *, final):
    x = x_ref[...]
    hb = _rms(x, g_ref[...], NORM_EPS).astype(bf16)
    gate = jnp.dot(hb, w1_ref[:, :FFN_HIDDEN], preferred_element_type=f32)
    up = jnp.dot(hb, w1_ref[:, FFN_HIDDEN:], preferred_element_type=f32)
    act = (_silu(gate) * up).astype(bf16)
    y = x + jnp.dot(act, w2_ref[...], preferred_element_type=f32)
    if final:
        y = _rms(y, fg_ref[...], NORM_EPS)
    out_ref[...] = y


def _ffn(x, layer_consts, fg, l, tm, final):
    B, T, D = x.shape
    row = pl.BlockSpec((None, tm, D), lambda b, j: (b, j, 0))
    return pl.pallas_call(
        functools.partial(_ffn_kernel, final=final),
        grid=(B, T // tm),
        in_specs=[row] + [_layer_spec(c, l) for c in layer_consts] + [_const_spec(fg)],
        out_specs=row,
        out_shape=jax.ShapeDtypeStruct((B, T, D), f32),
        compiler_params=_cparams(("parallel", "parallel")),
        name="ffn",
    )(x, *layer_consts, fg)


W_IN_A = RW_COLS + SSD_W + SSD_XBC
W_IN_UG = W_IN_A + SSD_HEADS
W_IN_COLS = W_IN_UG + S5_W + N_BRANCHES * D_MODEL


def _win_split_kernel(wt_ref, wa_ref, wug_ref, wdt_ref):
    wt = wt_ref[...]
    wa_ref[...] = wt[:W_IN_A, :].T.astype(bf16)
    wug_ref[...] = wt[W_IN_UG:, :].T.astype(bf16)
    dt_rows = jnp.concatenate([wt[W_IN_A:W_IN_UG, :],
                               jnp.zeros((DT_PAD - SSD_HEADS, wt.shape[1]), f32)], axis=0)
    wdt_ref[...] = dt_rows.T.astype(bf16)


def _win_split(w_in, rb):
    depth, D, cols = w_in.shape
    w_t = jnp.swapaxes(w_in, 1, 2)
    spec = lambda width: pl.BlockSpec((None, rb, width), lambda l, i: (l, i, 0))
    return pl.pallas_call(
        _win_split_kernel,
        grid=(depth, D // rb),
        in_specs=[pl.BlockSpec((None, cols, rb), lambda l, i: (l, 0, i))],
        out_specs=[spec(W_IN_A), spec(cols - W_IN_UG), spec(DT_PAD)],
        out_shape=[jax.ShapeDtypeStruct((depth, D, W_IN_A), bf16),
                   jax.ShapeDtypeStruct((depth, D, cols - W_IN_UG), bf16),
                   jax.ShapeDtypeStruct((depth, D, DT_PAD), bf16)],
        compiler_params=_cparams(("parallel", "parallel")),
        name="w_in_split",
    )(w_t)


def _rows(v):
    return v.reshape(v.shape[0], 1, -1).astype(f32)


def _s5_block_diag(w2d, row_group, col_group, dtype):
    tiled = jnp.tile(w2d, (1, 1, S5_GROUPS))
    r = jnp.arange(tiled.shape[1]) // row_group
    c = jnp.arange(tiled.shape[2]) // col_group
    return jnp.where((r[:, None] == c[None, :])[None], tiled, 0.0).astype(dtype)


def _tile_tm(T):
    for tm in (512, 256, 128):
        if T % tm == 0:
            return tm
    raise ValueError("sequence length must be a multiple of 128")


def kernel(x, norm_mix, w_in, rwkv_mu, rwkv_w0, rwkv_w_up, rwkv_a0, rwkv_a_up, rwkv_g_up, rwkv_k_k, rwkv_k_a, rwkv_r_k, rwkv_ln_g, rwkv_ln_b, proj_a, ssd_conv_w, ssd_conv_b, ssd_dt_bias, ssd_a_log, ssd_d, ssd_norm_g, proj_b, s5_lam_re, s5_lam_im, s5_log_step, s5_b_re, s5_b_im, s5_c_re, s5_c_im, s5_d, s5_glu_w, s5_glu_b, proj_c, w_out, norm_ffn, ffn_w_in, ffn_w_out, final_norm):
    B, T, D = x.shape
    depth = w_in.shape[0]
    rw_nc = RW_CHUNKS_PER_STEP
    assert D == D_MODEL and T % SSD_CHUNK == 0 and B % 8 == 0 and T % (rw_nc * RW_CHUNK) == 0
    tm = _tile_tm(T)
    hid = jnp.arange(2 * RW_N) // RW_N
    head_ones = (hid[:, None] == hid[None, :]).astype(bf16)
    c_idx = jnp.arange(rw_nc * RW_CHUNK)
    tril_rw = ((c_idx[:, None] >= c_idx[None, :])
               & (c_idx[:, None] // RW_CHUNK == c_idx[None, :] // RW_CHUNK)).astype(bf16)
    l_idx = jnp.arange(tm)
    tril_ssd = ((l_idx[:, None] >= l_idx[None, :])
                & (l_idx[:, None] // SSD_CHUNK == l_idx[None, :] // SSD_CHUNK)).astype(bf16)

    g_mix = _rows(norm_mix)
    assert w_in.shape[2] == W_IN_COLS
    w_a, w_ug, w_dt = _win_split(w_in, 256)
    zr = jnp.zeros((depth, RW_DECAY_RANK, RW_W), f32)
    w_wa = jnp.concatenate([jnp.concatenate([rwkv_w_up, zr], axis=2),
                            jnp.concatenate([zr, rwkv_a_up], axis=2)], axis=1).astype(bf16)
    rw_layer = (_rows(rwkv_mu), jnp.concatenate([_rows(rwkv_w0), _rows(rwkv_a0)], axis=2), w_wa,
                rwkv_g_up.astype(bf16), _rows(rwkv_k_k), _rows(rwkv_k_a), _rows(rwkv_r_k),
                _rows(rwkv_ln_g), _rows(rwkv_ln_b))
    ssd_layer = (
        jnp.concatenate([ssd_conv_w, jnp.zeros((depth, 8 - SSD_CONV, SSD_XBC), f32)], axis=1),
        _rows(ssd_conv_b),
        jnp.concatenate([_rows(ssd_dt_bias), jnp.zeros((depth, 1, DT_PAD - SSD_HEADS), f32)], axis=2),
        jnp.concatenate([_rows(ssd_a_log), jnp.zeros((depth, 1, DT_PAD - SSD_HEADS), f32)], axis=2),
        _rows(jnp.repeat(ssd_d, SSD_P, axis=1)),
        _rows(ssd_norm_g))
    b_rows = lambda b: jnp.swapaxes(b, 2, 3).reshape(depth, S5_W, S5_STATE)
    c_rows = lambda c: jnp.swapaxes(c, 2, 3).reshape(depth, S5_SN, S5_GROUP)
    nlt = S5_SN // LANES
    b_tiles = lambda b: _s5_block_diag(b_rows(b), S5_GROUP, S5_STATE, bf16).reshape(depth, S5_W, nlt, LANES)
    w_b = jnp.stack([b_tiles(s5_b_re), b_tiles(s5_b_im)],
                    axis=3).reshape(depth, S5_W, 2 * S5_SN)
    s5_layer = (
        _rows(s5_lam_re), _rows(s5_lam_im), _rows(jnp.repeat(s5_log_step, S5_STATE, axis=1)),
        w_b,
        _s5_block_diag(c_rows(s5_c_re), S5_STATE, S5_GROUP, f32),
        _s5_block_diag(c_rows(s5_c_im), S5_STATE, S5_GROUP, f32),
        _rows(s5_d), s5_glu_w.astype(bf16), _rows(s5_glu_b))
    merge_layer = (proj_a.astype(bf16), proj_b.astype(bf16), proj_c.astype(bf16), w_out.astype(bf16))
    ffn_layer = (_rows(norm_ffn), ffn_w_in.astype(bf16), ffn_w_out.astype(bf16))
    fg = final_norm.reshape(1, D).astype(f32)
    s5_tt = 32 if T % 32 == 0 else 8

    for l in range(depth):
        p, z, xbc, u, dt = _inproj(x, g_mix, w_a, w_ug, w_dt, l, tm)
        oa = _rwkv(p, rw_layer, (head_ones, tril_rw), l, rw_nc)
        oc = _s5(u, s5_layer, l, s5_tt)
        x = _merge(x, oa, oc, z, xbc, dt, g_mix, w_ug, merge_layer, ssd_layer, (tril_ssd,), l, tm)
        x = _ffn(x, ffn_layer, fg, l, tm, final=(l == depth - 1))
    return x
```

```python
import functools

import jax
import jax.numpy as jnp
from jax import lax
from jax.experimental import pallas as pl
from jax.experimental.pallas import tpu as pltpu

f32 = jnp.float32
bf16 = jnp.bfloat16

D_MODEL = 1024
RW_HEADS = 8
RW_N = 64
RW_W = RW_HEADS * RW_N
RW_DECAY_RANK = 64
RW_ICLR_RANK = 64
RW_GATE_RANK = 128
RW_COLS = 3 * RW_W + RW_DECAY_RANK + RW_ICLR_RANK + RW_GATE_RANK
RW_LN_EPS = 64e-5
RW_CHUNK = 64
RW_CHUNKS_PER_STEP = 4
SSD_HEADS = 8
SSD_P = 64
SSD_W = SSD_HEADS * SSD_P
SSD_GROUPS = 2
SSD_STATE = 128
SSD_CONV = 4
SSD_CHUNK = 128
SSD_XBC = SSD_W + 2 * SSD_GROUPS * SSD_STATE
SSD_NORM_EPS = 1e-5
SSD_GW = SSD_W // SSD_GROUPS
S5_GROUP = 16
S5_GROUPS = 32
S5_W = S5_GROUP * S5_GROUPS
S5_STATE = 64
S5_SN = S5_GROUPS * S5_STATE
N_BRANCHES = 3
FFN_HIDDEN = 2816
NORM_EPS = 1e-6
LANES = 128
DT_PAD = LANES

VMEM_LIMIT = 56 * 1024 * 1024


def _cparams(sem):
    return pltpu.CompilerParams(dimension_semantics=sem, vmem_limit_bytes=VMEM_LIMIT)


def _const_spec(arr):
    nd = arr.ndim
    return pl.BlockSpec(arr.shape, lambda *_: (0,) * nd, pipeline_mode=pl.Buffered(1))


def _layer_spec(arr, l):
    nd = arr.ndim - 1
    return pl.BlockSpec((None,) + arr.shape[1:], lambda *_: (l,) + (0,) * nd,
                        pipeline_mode=pl.Buffered(1))


def _mm(a, b):
    return jnp.dot(a.astype(bf16), b.astype(bf16), preferred_element_type=f32)


def _mm_nt(a, b):
    return lax.dot_general(a.astype(bf16), b.astype(bf16), (((1,), (1,)), ((), ())),
                           preferred_element_type=f32)


def _mm_tn(a, b):
    return lax.dot_general(a.astype(bf16), b.astype(bf16), (((0,), (0,)), ((), ())),
                           preferred_element_type=f32)


def _split(x, parts):
    out = []
    for _ in range(parts - 1):
        hi = x.astype(bf16)
        out.append(hi)
        x = x - hi.astype(f32)
    out.append(x.astype(bf16))
    return out


def _mm_exact_rhs(a, b_bf16, parts):
    acc = None
    for piece in _split(a, parts):
        t = jnp.dot(piece, b_bf16, preferred_element_type=f32)
        acc = t if acc is None else acc + t
    return acc


def _mm_exact_lhs(a_bf16, b, parts):
    acc = None
    for piece in _split(b, parts):
        t = jnp.dot(a_bf16, piece, preferred_element_type=f32)
        acc = t if acc is None else acc + t
    return acc


def _rms(x, g, eps):
    return x * lax.rsqrt(jnp.mean(x * x, axis=-1, keepdims=True) + eps) * g


def _sigmoid(x):
    return 1.0 / (1.0 + jnp.exp(-x))


def _softplus(x):
    return jnp.maximum(x, 0.0) + jnp.log(1.0 + jnp.exp(-jnp.abs(x)))


def _silu(x):
    return x * _sigmoid(x)


def _inproj_kernel(x_ref, g_ref, wa_ref, wu_ref, wdt_ref, p_ref, z_ref, xbc_ref, u_ref, dt_ref):
    hb = _rms(x_ref[...], g_ref[...], NORM_EPS).astype(bf16)
    o = 0
    for ref, width in ((p_ref, RW_COLS), (z_ref, SSD_W), (xbc_ref, SSD_XBC)):
        ref[...] = jnp.dot(hb, wa_ref[:, o:o + width], preferred_element_type=f32)
        o += width
    u_ref[...] = jnp.dot(hb, wu_ref[...], preferred_element_type=f32)
    dt_ref[...] = jnp.dot(hb, wdt_ref[...], preferred_element_type=f32)


def _col_block_spec(arr, l, width, idx):
    return pl.BlockSpec((None, arr.shape[1], width), lambda *_: (l, 0, idx), pipeline_mode=pl.Buffered(1))


def _inproj(x, g, w_a, w_ug, w_dt, l, tm):
    B, T, D = x.shape
    nt = T // tm
    row = lambda width: pl.BlockSpec((None, tm, width), lambda b, j: (b, j, 0))
    return pl.pallas_call(
        _inproj_kernel,
        grid=(B, nt),
        in_specs=[row(D), _layer_spec(g, l), _layer_spec(w_a, l), _col_block_spec(w_ug, l, S5_W, 0),
                  _layer_spec(w_dt, l)],
        out_specs=[row(RW_COLS), row(SSD_W), row(SSD_XBC), row(S5_W), row(DT_PAD)],
        out_shape=[jax.ShapeDtypeStruct((B, T, RW_COLS), f32),
                   jax.ShapeDtypeStruct((B, T, SSD_W), f32),
                   jax.ShapeDtypeStruct((B, T, SSD_XBC), f32),
                   jax.ShapeDtypeStruct((B, T, S5_W), f32),
                   jax.ShapeDtypeStruct((B, T, DT_PAD), f32)],
        compiler_params=_cparams(("parallel", "parallel")),
        name="inproj",
    )(x, g, w_a, w_ug, w_dt)


def _stack_heads(x):
    lane = lax.broadcasted_iota(jnp.int32, x.shape, 1)
    lo = lane < RW_N
    return jnp.concatenate([jnp.where(lo, x, 0.0), jnp.where(lo, 0.0, x)], axis=0)


def _head_sums(x, ones_pair):
    R = x.shape[0]
    PW = 2 * RW_N
    st = jnp.concatenate([x[:, q * PW:(q + 1) * PW] for q in range(RW_W // PW)], axis=0)
    s = _mm_exact_rhs(st, ones_pair, 2)
    return jnp.concatenate([s[q * R:(q + 1) * R] for q in range(RW_W // PW)], axis=1)


def _rwkv_kernel(p_ref, mu_ref, w0a0_ref, wwa_ref, gup_ref, kk_ref, ka_ref, rk_ref, lng_ref, lnb_ref,
                 ones_ref, tril_ref, o_ref, prev_ref, h_ref, *, nc):
    C = RW_CHUNK
    TB = nc * C
    PW = 2 * RW_N
    NP = RW_HEADS // 2

    @pl.when(pl.program_id(1) == 0)
    def _():
        prev_ref[...] = jnp.zeros_like(prev_ref)
        h_ref[...] = jnp.zeros_like(h_ref)

    p = p_ref[...]
    row = lax.broadcasted_iota(jnp.int32, p.shape, 0)
    shifted = jnp.where(row == 0, prev_ref[...], pltpu.roll(p, 1, 0))
    prev_ref[...] = p[TB - 1:TB, :]
    pm = p + (shifted - p) * mu_ref[...]
    r = pm[:, 0:RW_W]
    k = pm[:, RW_W:2 * RW_W]
    v = pm[:, 2 * RW_W:3 * RW_W]
    wa_in = pm[:, 3 * RW_W:3 * RW_W + PW]
    gd = pm[:, 3 * RW_W + PW:RW_COLS]
    lane = lax.broadcasted_iota(jnp.int32, wa_in.shape, 1)
    wa_in = jnp.where(lane < RW_DECAY_RANK, jnp.tanh(wa_in), wa_in)
    wa = w0a0_ref[...] + _mm(wa_in, wwa_ref[...])
    w_log = -_softplus(-wa[:, :RW_W]) - 0.5
    lw = -jnp.exp(w_log)
    a = _sigmoid(wa[:, RW_W:])
    g = _mm(_sigmoid(gd), gup_ref[...])

    ones = ones_ref[...]
    kk = k * kk_ref[...]
    kk = kk * lax.rsqrt(jnp.maximum(_head_sums(kk * kk, ones), 1e-24))
    k2 = k * (1.0 + (a - 1.0) * ka_ref[...])

    lc = _mm_exact_lhs(tril_ref[...], lw, 2)
    lc_last = jnp.concatenate(
        [jnp.broadcast_to(lc[(c + 1) * C - 1:(c + 1) * C, :], (C, RW_W)) for c in range(nc)], axis=0)
    e_pos = jnp.exp(lc)
    e_neg = jnp.exp(-lc)
    e_end = jnp.exp(lc_last - lc)
    kka = kk * a
    r_hat = r * e_pos
    a_hat = -kk * jnp.exp(lc - lw)
    b_hat = kka * e_neg
    k_hat = k2 * e_neg
    b_til = kka * e_end
    k_til = k2 * e_end

    ri = lax.broadcasted_iota(jnp.int32, (2 * C, 2 * PW), 0)
    ci = lax.broadcasted_iota(jnp.int32, (2 * C, 2 * PW), 1)
    tri = (ri % C) + jnp.where(ri < C, 0, 1) > (ci % C)
    ei = lax.broadcasted_iota(jnp.int32, (PW, PW), 0)
    ej = lax.broadcasted_iota(jnp.int32, (PW, PW), 1)
    eye = ei == ej
    blk = (ei // RW_N) == (ej // RW_N)
    eye_c = jnp.where(lax.broadcasted_iota(jnp.int32, (C, PW), 0)
                      == lax.broadcasted_iota(jnp.int32, (C, PW), 1) % C, 1.0, 0.0)

    inst = [(c, q) for c in range(nc) for q in range(NP)]
    every = range(len(inst))

    def pair(t):
        return [t[c * C:(c + 1) * C, q * PW:(q + 1) * PW] for c, q in inst]

    ac, rc = pair(a_hat), pair(r_hat)
    am, bm, km = map(_stack_heads, ac), map(_stack_heads, pair(b_hat)), map(_stack_heads, pair(k_hat))
    am, bm, km = list(am), list(bm), list(km)
    btm, ktm, vm = (list(map(_stack_heads, pair(t))) for t in (b_til, k_til, v))
    a_all = [_mm_nt(jnp.concatenate([ac[i], rc[i]], axis=0), jnp.concatenate([bm[i], km[i]], axis=0))
             for i in every]
    a_all = [jnp.where(tri, t, 0.0) for t in a_all]
    pw = [t[:C, :PW] for t in a_all]
    t_inv = [eye_c + t for t in pw]
    pw = [_mm(t, _stack_heads(t)) for t in pw]
    for _ in range(4):
        both = [_mm(jnp.concatenate([pw[i], t_inv[i]], axis=0), _stack_heads(pw[i])) for i in every]
        t_inv = [t_inv[i] + both[i][C:] for i in every]
        pw = [t[:C] for t in both]
    t_inv = [t_inv[i] + _mm(t_inv[i], _stack_heads(pw[i])) for i in every]
    av = [_mm(a_all[i][:, PW:], vm[i]) for i in every]
    pq = [_mm(t_inv[i], jnp.concatenate([am[i], _stack_heads(av[i][:C])], axis=1)) for i in every]
    pqm = [jnp.concatenate([_stack_heads(t[:, :PW]), _stack_heads(t[:, PW:])], axis=1) for t in pq]
    xy = [_mm(a_all[i][C:, :PW], pqm[i]) for i in every]
    mk = [_mm_tn(btm[i], pqm[i][:, :PW]) for i in every]
    gk = [_mm_tn(jnp.concatenate([btm[i], ktm[i]], axis=0), jnp.concatenate([pqm[i][:, PW:], vm[i]], axis=0))
          for i in every]
    mx = [jnp.concatenate([mk[i], rc[i] + xy[i][:, :PW]], axis=0) for i in every]
    ym = [xy[i][:, PW:] + av[i][C:] for i in every]
    gam_col = [jnp.sum(jnp.where(eye, jnp.exp(lc[(c + 1) * C - 1:(c + 1) * C, q * PW:(q + 1) * PW]), 0.0),
                       axis=1, keepdims=True) for c, q in inst]

    hs = [h_ref[q] for q in range(NP)]
    o_rows = []
    for c in range(nc):
        seq = [_mm(mx[c * NP + q], hs[q]) for q in range(NP)]
        hs = [jnp.where(blk, gam_col[c * NP + q] * hs[q] + seq[q][:PW] + gk[c * NP + q], 0.0) for q in range(NP)]
        o_rows.append(jnp.concatenate([seq[q][PW:] + ym[c * NP + q] for q in range(NP)], axis=1))
    for q in range(NP):
        h_ref[q] = hs[q]
    o = jnp.concatenate(o_rows, axis=0)

    inv_n = 1.0 / RW_N
    mean = _head_sums(o, ones) * inv_n
    oc = o - mean
    var = _head_sums(oc * oc, ones) * inv_n
    o = oc * lax.rsqrt(var + RW_LN_EPS) * lng_ref[...] + lnb_ref[...]
    bonus = _head_sums(r * k2 * rk_ref[...], ones) * v
    o_ref[...] = (o + bonus) * g


def _rwkv(p, layer_consts, shared_consts, l, nc):
    B, T, _ = p.shape
    TB = nc * RW_CHUNK
    return pl.pallas_call(
        functools.partial(_rwkv_kernel, nc=nc),
        grid=(B, T // TB),
        in_specs=[pl.BlockSpec((None, TB, RW_COLS), lambda b, j: (b, j, 0))]
                 + [_layer_spec(c, l) for c in layer_consts] + [_const_spec(c) for c in shared_consts],
        out_specs=pl.BlockSpec((None, TB, RW_W), lambda b, j: (b, j, 0)),
        out_shape=jax.ShapeDtypeStruct((B, T, RW_W), f32),
        scratch_shapes=[pltpu.VMEM((1, RW_COLS), f32),
                        pltpu.VMEM((RW_HEADS // 2, 2 * RW_N, 2 * RW_N), f32)],
        compiler_params=_cparams(("parallel", "arbitrary")),
        name="rwkv7",
    )(p, *layer_consts, *shared_consts)


def _ssd_stages(z_ref, xbc_ref, dt_ref, cw_ref, cb_ref, dtb_ref, alog_ref, dskip_ref, ng_ref,
                tril_ref, hist_ref, st_ref, result, *, nc):
    L = SSD_CHUNK
    TB = nc * L
    HIST = 8

    @pl.when(pl.program_id(1) == 0)
    def _():
        hist_ref[...] = jnp.zeros_like(hist_ref)
        st_ref[...] = jnp.zeros_like(st_ref)

    x_in = xbc_ref[...]
    hist = hist_ref[...]
    row8 = lax.broadcasted_iota(jnp.int32, (HIST, SSD_XBC), 0)
    conv = cb_ref[...] + cw_ref[SSD_CONV - 1:SSD_CONV, :] * x_in
    for back in range(1, SSD_CONV):
        rolled = pltpu.roll(x_in, back, 0)
        head = jnp.where(row8 < back, pltpu.roll(hist, back, 0), rolled[0:HIST])
        shifted = jnp.concatenate([head, rolled[HIST:]], axis=0)
        conv = conv + cw_ref[SSD_CONV - 1 - back:SSD_CONV - back, :] * shifted
    hist_ref[...] = x_in[TB - HIST:TB, :]
    yield
    xbc = _silu(conv)
    xs = xbc[:, :SSD_W]
    bmat = xbc[:, SSD_W:SSD_W + SSD_GROUPS * SSD_STATE]
    cmat = xbc[:, SSD_W + SSD_GROUPS * SSD_STATE:]
    yield

    dt = _softplus(dt_ref[...] + dtb_ref[...])
    da = dt * -jnp.exp(alog_ref[...])
    acs = _mm_exact_lhs(tril_ref[...], da, 2)
    yield
    dt_rep = jnp.concatenate([jnp.broadcast_to(dt[:, h:h + 1], (TB, LANES)) for h in range(SSD_HEADS)], axis=1)
    acs_rep = jnp.concatenate([jnp.broadcast_to(acs[:, h:h + 1], (TB, LANES)) for h in range(SSD_HEADS)], axis=1)

    lane = lax.broadcasted_iota(jnp.int32, (TB, LANES), 1)
    lo = lane < SSD_P

    def pair64(rep, q):
        return jnp.where(lo, rep[:, (2 * q) * LANES:(2 * q + 1) * LANES],
                         rep[:, (2 * q + 1) * LANES:(2 * q + 2) * LANES])

    dt64 = jnp.concatenate([pair64(dt_rep, q) for q in range(SSD_HEADS // 2)], axis=1)
    acs64 = jnp.concatenate([pair64(acs_rep, q) for q in range(SSD_HEADS // 2)], axis=1)
    xdt = xs * dt64
    last = [acs64[(c + 1) * L - 1:(c + 1) * L, :] for c in range(nc)]
    acs_last = jnp.concatenate([jnp.broadcast_to(t, (L, SSD_W)) for t in last], axis=0)
    xdec = xdt * jnp.exp(acs_last - acs64)
    e_acs = jnp.exp(acs64)
    chunk_decay = [jnp.exp(t) for t in last]
    yield

    li = lax.broadcasted_iota(jnp.int32, (L, L), 0)
    si = lax.broadcasted_iota(jnp.int32, (L, L), 1)
    causal = li >= si
    eye = li == si
    glane = lax.broadcasted_iota(jnp.int32, (L, SSD_GW), 1)
    hpg = SSD_HEADS // SSD_GROUPS
    G = SSD_GROUPS
    inst = [(c, gi) for c in range(nc) for gi in range(G)]
    every = range(len(inst))
    rows = lambda c: slice(c * L, (c + 1) * L)
    gsl = lambda gi: slice(gi * SSD_GW, (gi + 1) * SSD_GW)
    ssl = lambda gi: slice(gi * SSD_STATE, (gi + 1) * SSD_STATE)
    bg = [bmat[rows(c), ssl(gi)] for c, gi in inst]
    cg = [cmat[rows(c), ssl(gi)] for c, gi in inst]
    scores = [_mm_nt(cg[i], bg[i]) for i in every]
    yield
    wcat, xst = [], []
    for i, (c, gi) in enumerate(inst):
        if i == len(inst) // 2:
            yield
        xg = xdt[rows(c), gsl(gi)]
        wm, xm = [], []
        for j in range(hpg):
            hd = gi * hpg + j
            col = acs_rep[rows(c), hd * LANES:(hd + 1) * LANES]
            rowv = jnp.sum(jnp.where(eye, col, 0.0), axis=0, keepdims=True)
            wm.append(jnp.where(causal, scores[i] * jnp.exp(col - rowv), 0.0))
            xm.append(jnp.where((glane // SSD_P) == j, xg, 0.0))
        wcat.append(jnp.concatenate(wm, axis=1))
        xst.append(jnp.concatenate(xm, axis=0))
    yield
    y_diag = [_mm(wcat[i], xst[i]) for i in every]
    upd = [_mm_tn(bg[i], xdec[rows(c), gsl(gi)]) for i, (c, gi) in enumerate(inst)]
    yield
    st = [st_ref[gi] for gi in range(G)]
    y_rows = []
    for c in range(nc):
        y_off = [_mm(cg[c * G + gi], st[gi]) * e_acs[rows(c), gsl(gi)] for gi in range(G)]
        st = [chunk_decay[c][:, gsl(gi)] * st[gi] + upd[c * G + gi] for gi in range(G)]
        y_rows.append(jnp.concatenate([y_diag[c * G + gi] + y_off[gi] for gi in range(G)], axis=1))
    for gi in range(G):
        st_ref[gi] = st[gi]
    yield
    y = jnp.concatenate(y_rows, axis=0) + xs * dskip_ref[...]
    y = y * _silu(z_ref[...])
    outs = []
    for gi in range(SSD_GROUPS):
        yg = y[:, gi * SSD_GW:(gi + 1) * SSD_GW]
        outs.append(yg * lax.rsqrt(jnp.mean(yg * yg, axis=-1, keepdims=True) + SSD_NORM_EPS))
    result.append(jnp.concatenate(outs, axis=1) * ng_ref[...])


def _s5_kernel(u_ref, lre_ref, lim_ref, lstep_ref, wb_ref, cre_ref, cim_ref, d_ref,
               gw_ref, gb_ref, perm_ref, permt_ref, o_ref, sre_ref, sim_ref, bur_ref, bui_ref, par_ref,
               cpr_ref, cpi_ref, *, nb, tt):
    @pl.when(pl.program_id(0) == 0)
    def _():
        sre_ref[...] = jnp.zeros_like(sre_ref)
        sim_ref[...] = jnp.zeros_like(sim_ref)
        lr = lre_ref[...]
        li = lim_ref[...]
        step = jnp.exp(lstep_ref[...])
        mag = jnp.exp(lr * step)
        a_re = mag * jnp.cos(li * step)
        a_im = mag * jnp.sin(li * step)
        den = lr * lr + li * li
        coef_re = ((a_re - 1.0) * lr + a_im * li) / den
        coef_im = (a_im * lr - (a_re - 1.0) * li) / den
        par_ref[0:1, :] = a_re
        par_ref[1:2, :] = a_im
        ei = lax.broadcasted_iota(jnp.int32, (LANES, LANES), 0)
        ej = lax.broadcasted_iota(jnp.int32, (LANES, LANES), 1)
        eye = ei == ej
        for j in range(S5_SN // LANES):
            ls = slice(j * LANES, (j + 1) * LANES)
            kr = jnp.sum(jnp.where(eye, coef_re[:, ls], 0.0), axis=1, keepdims=True)
            ki = jnp.sum(jnp.where(eye, coef_im[:, ls], 0.0), axis=1, keepdims=True)
            c_re = cre_ref[ls, :]
            c_im = cim_ref[ls, :]
            cpr_ref[ls, :] = (c_re * kr - c_im * ki).astype(bf16)
            cpi_ref[ls, :] = (c_re * ki + c_im * kr).astype(bf16)

    ab_re = par_ref[0:1, :]
    ab_im = par_ref[1:2, :]

    ub = jnp.dot(perm_ref[...], u_ref[...].reshape(nb * tt, S5_W).astype(bf16),
                 preferred_element_type=f32).astype(bf16)
    u = ub.astype(f32)
    LB = LANES // S5_GROUP * S5_STATE
    NBLK = S5_SN // LB
    in_per_tile = LANES // S5_STATE * S5_GROUP
    ys = [None] * NBLK

    def project_in(jb):
        for j in range(jb * LB // LANES, (jb + 1) * LB // LANES):
            us = slice(j * in_per_tile // LANES * LANES, (j * in_per_tile // LANES + 1) * LANES)
            ls = slice(j * LANES, (j + 1) * LANES)
            xri = jnp.dot(ub[:, us], wb_ref[us, 2 * j * LANES:2 * (j + 1) * LANES], preferred_element_type=f32)
            bur_ref[:, ls] = xri[:, :LANES]
            bui_ref[:, ls] = xri[:, LANES:]
            yield

    def project_out(m):
        ks = slice(m * LB, (m + 1) * LB)
        os_ = slice(m * LANES, (m + 1) * LANES)
        y_re = jnp.dot(bur_ref[:, ks].astype(bf16), cpr_ref[ks, os_], preferred_element_type=f32)
        yield
        ys[m] = y_re - jnp.dot(bui_ref[:, ks].astype(bf16), cpi_ref[ks, os_], preferred_element_type=f32)
        yield

    def chain(*gens):
        for gen in gens:
            yield from gen

    for _ in project_in(0):
        pass
    for jb in range(NBLK):
        side = chain(project_in(jb + 1) if jb + 1 < NBLK else (), project_out(jb - 1) if jb > 0 else ())
        ls = slice(jb * LB, (jb + 1) * LB)
        ar = jnp.broadcast_to(ab_re[:, ls], (nb, LB))
        ai = jnp.broadcast_to(ab_im[:, ls], (nb, LB))
        s_re = sre_ref[:, ls]
        s_im = sim_ref[:, ls]
        for t in range(tt):
            rows = slice(t * nb, (t + 1) * nb)
            s_re, s_im = (ar * s_re - ai * s_im + bur_ref[rows, ls], ar * s_im + ai * s_re + bui_ref[rows, ls])
            bur_ref[rows, ls] = s_re
            bui_ref[rows, ls] = s_im
            if t % 5 == 4:
                next(side, None)
        sre_ref[:, ls] = s_re
        sim_ref[:, ls] = s_im
        for _ in side:
            pass
    for _ in project_out(NBLK - 1):
        pass
    y = jnp.concatenate(ys, axis=1) + d_ref[...] * u
    zg = 0.5 * y * (1.0 + jnp.tanh(0.7978845608028654 * (y + 0.044715 * (y * y * y))))
    out = (zg * _sigmoid(_mm(zg, gw_ref[...]) + gb_ref[...])).astype(bf16)
    o_ref[...] = jnp.dot(permt_ref[...], out, preferred_element_type=f32).astype(bf16).reshape(nb, tt, S5_W)


def _s5(u, layer_consts, l, tt):
    B, T, _ = u.shape
    blk = tt * B
    src = jnp.arange(blk)
    perm = (((src % B) * tt + src // B)[:, None] == src[None, :]).astype(bf16)
    shared = (perm, perm.T)
    spec = pl.BlockSpec((B, tt, S5_W), lambda i: (0, i, 0))
    return pl.pallas_call(
        functools.partial(_s5_kernel, nb=B, tt=tt),
        grid=(T // tt,),
        in_specs=[spec] + [_layer_spec(c, l) for c in layer_consts] + [_const_spec(c) for c in shared],
        out_specs=spec,
        out_shape=jax.ShapeDtypeStruct((B, T, S5_W), bf16),
        scratch_shapes=[pltpu.VMEM((B, S5_SN), f32), pltpu.VMEM((B, S5_SN), f32),
                        pltpu.VMEM((blk, S5_SN), f32), pltpu.VMEM((blk, S5_SN), f32),
                        pltpu.VMEM((8, S5_SN), f32),
                        pltpu.VMEM((S5_SN, S5_W), bf16), pltpu.VMEM((S5_SN, S5_W), bf16)],
        compiler_params=_cparams(("arbitrary",)),
        name="s5",
    )(u, *layer_consts, *shared)


GATE_BLOCKS = N_BRANCHES * D_MODEL // S5_W


N_MERGE_W = 4


def _merge_kernel(x_ref, oa_ref, oc_ref, z_ref, xbc_ref, dt_ref, g_ref, *rest, nc):
    wg_refs = rest[:GATE_BLOCKS]
    pa_ref, pb_ref, pc_ref, wo_ref = rest[GATE_BLOCKS:GATE_BLOCKS + N_MERGE_W]
    ssd_refs = rest[GATE_BLOCKS + N_MERGE_W:-3]
    out_ref, hist_ref, st_ref = rest[-3:]
    per = GATE_BLOCKS // N_BRANCHES
    ssd_out = []
    ssd = _ssd_stages(z_ref, xbc_ref, dt_ref, *ssd_refs, hist_ref, st_ref, ssd_out, nc=nc)
    ya = jnp.dot(oa_ref[...].astype(bf16), pa_ref[...], preferred_element_type=f32)
    next(ssd)
    x = x_ref[...]
    hb = _rms(x, g_ref[...], NORM_EPS).astype(bf16)
    yc = jnp.dot(oc_ref[...].astype(bf16), pc_ref[...], preferred_element_type=f32)
    next(ssd, None)
    logit_blocks = []
    for w in wg_refs:
        logit_blocks.append(jnp.dot(hb, w[...], preferred_element_type=f32))
        next(ssd, None)
    gates = [_sigmoid(jnp.concatenate(logit_blocks[i * per:(i + 1) * per], axis=1)) for i in range(N_BRANCHES)]
    for _ in ssd:
        pass
    merged = gates[0] * ya + gates[2] * yc
    yb = jnp.dot(ssd_out[0].astype(bf16), pb_ref[...], preferred_element_type=f32)
    merged = merged + gates[1] * yb
    out_ref[...] = x + jnp.dot(merged.astype(bf16), wo_ref[...], preferred_element_type=f32)


def _merge(x, oa, oc, z, xbc, dt, g, w_ug, layer_consts, ssd_layer, ssd_shared, l, tm):
    B, T, D = x.shape
    nc = tm // SSD_CHUNK
    row = lambda width: pl.BlockSpec((None, tm, width), lambda b, j: (b, j, 0))
    gate_specs = [_col_block_spec(w_ug, l, S5_W, 1 + k) for k in range(GATE_BLOCKS)]
    return pl.pallas_call(
        functools.partial(_merge_kernel, nc=nc),
        grid=(B, T // tm),
        in_specs=[row(D), row(RW_W), row(S5_W), row(SSD_W), row(SSD_XBC), row(DT_PAD), _layer_spec(g, l)]
                 + gate_specs + [_layer_spec(c, l) for c in layer_consts]
                 + [_layer_spec(c, l) for c in ssd_layer] + [_const_spec(c) for c in ssd_shared],
        out_specs=row(D),
        out_shape=jax.ShapeDtypeStruct((B, T, D), f32),
        scratch_shapes=[pltpu.VMEM((8, SSD_XBC), f32),
                        pltpu.VMEM((SSD_GROUPS, SSD_STATE, SSD_GW), f32)],
        compiler_params=_cparams(("parallel", "arbitrary")),
        name="merge_ssd",
    )(x, oa, oc, z, xbc, dt, g, *([w_ug] * GATE_BLOCKS), *layer_consts, *ssd_layer, *ssd_shared)


def _ffn_kernel(x_ref, g_ref, w1_ref, w2_ref, fg_ref, out_ref, *, final):
    x = x_ref[...]
    hb = _rms(x, g_ref[...], NORM_EPS).astype(bf16)
    gate = jnp.dot(hb, w1_ref[:, :FFN_HIDDEN], preferred_element_type=f32)
    up = jnp.dot(hb, w1_ref[:, FFN_HIDDEN:], preferred_element_type=f32)
    act = (_silu(gate) * up).astype(bf16)
    y = x + jnp.dot(act, w2_ref[...], preferred_element_type=f32)
    if final:
        y = _rms(y, fg_ref[...], NORM_EPS)
    out_ref[...] = y


def _ffn(x, layer_consts, fg, l, tm, final):
    B, T, D = x.shape
    row = pl.BlockSpec((None, tm, D), lambda b, j: (b, j, 0))
    return pl.pallas_call(
        functools.partial(_ffn_kernel, final=final),
        grid=(B, T // tm),
        in_specs=[row] + [_layer_spec(c, l) for c in layer_consts] + [_const_spec(fg)],
        out_specs=row,
        out_shape=jax.ShapeDtypeStruct((B, T, D), f32),
        compiler_params=_cparams(("parallel", "parallel")),
        name="ffn",
    )(x, *layer_consts, fg)


W_IN_A = RW_COLS + SSD_W + SSD_XBC
W_IN_UG = W_IN_A + SSD_HEADS
W_IN_COLS = W_IN_UG + S5_W + N_BRANCHES * D_MODEL


def _win_split_kernel(wt_ref, wa_ref, wug_ref, wdt_ref):
    wt = wt_ref[...]
    wa_ref[...] = wt[:W_IN_A, :].T.astype(bf16)
    wug_ref[...] = wt[W_IN_UG:, :].T.astype(bf16)
    dt_rows = jnp.concatenate([wt[W_IN_A:W_IN_UG, :],
                               jnp.zeros((DT_PAD - SSD_HEADS, wt.shape[1]), f32)], axis=0)
    wdt_ref[...] = dt_rows.T.astype(bf16)


def _win_split(w_in, rb):
    depth, D, cols = w_in.shape
    w_t = jnp.swapaxes(w_in, 1, 2)
    spec = lambda width: pl.BlockSpec((None, rb, width), lambda l, i: (l, i, 0))
    return pl.pallas_call(
        _win_split_kernel,
        grid=(depth, D // rb),
        in_specs=[pl.BlockSpec((None, cols, rb), lambda l, i: (l, 0, i))],
        out_specs=[spec(W_IN_A), spec(cols - W_IN_UG), spec(DT_PAD)],
        out_shape=[jax.ShapeDtypeStruct((depth, D, W_IN_A), bf16),
                   jax.ShapeDtypeStruct((depth, D, cols - W_IN_UG), bf16),
                   jax.ShapeDtypeStruct((depth, D, DT_PAD), bf16)],
        compiler_params=_cparams(("parallel", "parallel")),
        name="w_in_split",
    )(w_t)


def _rows(v):
    return v.reshape(v.shape[0], 1, -1).astype(f32)


def _s5_block_diag(w2d, row_group, col_group, dtype):
    tiled = jnp.tile(w2d, (1, 1, S5_GROUPS))
    r = jnp.arange(tiled.shape[1]) // row_group
    c = jnp.arange(tiled.shape[2]) // col_group
    return jnp.where((r[:, None] == c[None, :])[None], tiled, 0.0).astype(dtype)


def _tile_tm(T):
    for tm in (512, 256, 128):
        if T % tm == 0:
            return tm
    raise ValueError("sequence length must be a multiple of 128")


def kernel(x, norm_mix, w_in, rwkv_mu, rwkv_w0, rwkv_w_up, rwkv_a0, rwkv_a_up, rwkv_g_up, rwkv_k_k, rwkv_k_a, rwkv_r_k, rwkv_ln_g, rwkv_ln_b, proj_a, ssd_conv_w, ssd_conv_b, ssd_dt_bias, ssd_a_log, ssd_d, ssd_norm_g, proj_b, s5_lam_re, s5_lam_im, s5_log_step, s5_b_re, s5_b_im, s5_c_re, s5_c_im, s5_d, s5_glu_w, s5_glu_b, proj_c, w_out, norm_ffn, ffn_w_in, ffn_w_out, final_norm):
    B, T, D = x.shape
    depth = w_in.shape[0]
    rw_nc = RW_CHUNKS_PER_STEP
    assert D == D_MODEL and T % SSD_CHUNK == 0 and B % 8 == 0 and T % (rw_nc * RW_CHUNK) == 0
    tm = _tile_tm(T)
    hid = jnp.arange(2 * RW_N) // RW_N
    head_ones = (hid[:, None] == hid[None, :]).astype(bf16)
    c_idx = jnp.arange(rw_nc * RW_CHUNK)
    tril_rw = ((c_idx[:, None] >= c_idx[None, :])
               & (c_idx[:, None] // RW_CHUNK == c_idx[None, :] // RW_CHUNK)).astype(bf16)
    l_idx = jnp.arange(tm)
    tril_ssd = ((l_idx[:, None] >= l_idx[None, :])
                & (l_idx[:, None] // SSD_CHUNK == l_idx[None, :] // SSD_CHUNK)).astype(bf16)

    g_mix = _rows(norm_mix)
    assert w_in.shape[2] == W_IN_COLS
    w_a, w_ug, w_dt = _win_split(w_in, 256)
    zr = jnp.zeros((depth, RW_DECAY_RANK, RW_W), f32)
    w_wa = jnp.concatenate([jnp.concatenate([rwkv_w_up, zr], axis=2),
                            jnp.concatenate([zr, rwkv_a_up], axis=2)], axis=1).astype(bf16)
    rw_layer = (_rows(rwkv_mu), jnp.concatenate([_rows(rwkv_w0), _rows(rwkv_a0)], axis=2), w_wa,
                rwkv_g_up.astype(bf16), _rows(rwkv_k_k), _rows(rwkv_k_a), _rows(rwkv_r_k),
                _rows(rwkv_ln_g), _rows(rwkv_ln_b))
    ssd_layer = (
        jnp.concatenate([ssd_conv_w, jnp.zeros((depth, 8 - SSD_CONV, SSD_XBC), f32)], axis=1),
        _rows(ssd_conv_b),
        jnp.concatenate([_rows(ssd_dt_bias), jnp.zeros((depth, 1, DT_PAD - SSD_HEADS), f32)], axis=2),
        jnp.concatenate([_rows(ssd_a_log), jnp.zeros((depth, 1, DT_PAD - SSD_HEADS), f32)], axis=2),
        _rows(jnp.repeat(ssd_d, SSD_P, axis=1)),
        _rows(ssd_norm_g))
    b_rows = lambda b: jnp.swapaxes(b, 2, 3).reshape(depth, S5_W, S5_STATE)
    c_rows = lambda c: jnp.swapaxes(c, 2, 3).reshape(depth, S5_SN, S5_GROUP)
    nlt = S5_SN // LANES
    b_tiles = lambda b: _s5_block_diag(b_rows(b), S5_GROUP, S5_STATE, bf16).reshape(depth, S5_W, nlt, LANES)
    w_b = jnp.stack([b_tiles(s5_b_re), b_tiles(s5_b_im)],
                    axis=3).reshape(depth, S5_W, 2 * S5_SN)
    s5_layer = (
        _rows(s5_lam_re), _rows(s5_lam_im), _rows(jnp.repeat(s5_log_step, S5_STATE, axis=1)),
        w_b,
        _s5_block_diag(c_rows(s5_c_re), S5_STATE, S5_GROUP, f32),
        _s5_block_diag(c_rows(s5_c_im), S5_STATE, S5_GROUP, f32),
        _rows(s5_d), s5_glu_w.astype(bf16), _rows(s5_glu_b))
    merge_layer = (proj_a.astype(bf16), proj_b.astype(bf16), proj_c.astype(bf16), w_out.astype(bf16))
    ffn_layer = (_rows(norm_ffn), ffn_w_in.astype(bf16), ffn_w_out.astype(bf16))
    fg = final_norm.reshape(1, D).astype(f32)
    s5_tt = 32 if T % 32 == 0 else 8

    for l in range(depth):
        p, z, xbc, u, dt = _inproj(x, g_mix, w_a, w_ug, w_dt, l, tm)
        oa = _rwkv(p, rw_layer, (head_ones, tril_rw), l, rw_nc)
        oc = _s5(u, s5_layer, l, s5_tt)
        x = _merge(x, oa, oc, z, xbc, dt, g_mix, w_ug, merge_layer, ssd_layer, (tril_ssd,), l, tm)
        x = _ffn(x, ffn_layer, fg, l, tm, final=(l == depth - 1))
    return x
```

```python
import functools

import jax
import jax.numpy as jnp
from jax import lax
from jax.experimental import pallas as pl
from jax.experimental.pallas import tpu as pltpu

f32 = jnp.float32
bf16 = jnp.bfloat16

D_MODEL = 1024
RW_HEADS = 8
RW_N = 64
RW_W = RW_HEADS * RW_N
RW_DECAY_RANK = 64
RW_ICLR_RANK = 64
RW_GATE_RANK = 128
RW_COLS = 3 * RW_W + RW_DECAY_RANK + RW_ICLR_RANK + RW_GATE_RANK
RW_LN_EPS = 64e-5
RW_CHUNK = 64
RW_CHUNKS_PER_STEP = 4
SSD_HEADS = 8
SSD_P = 64
SSD_W = SSD_HEADS * SSD_P
SSD_GROUPS = 2
SSD_STATE = 128
SSD_CONV = 4
SSD_CHUNK = 128
SSD_XBC = SSD_W + 2 * SSD_GROUPS * SSD_STATE
SSD_NORM_EPS = 1e-5
SSD_GW = SSD_W // SSD_GROUPS
S5_GROUP = 16
S5_GROUPS = 32
S5_W = S5_GROUP * S5_GROUPS
S5_STATE = 64
S5_SN = S5_GROUPS * S5_STATE
N_BRANCHES = 3
FFN_HIDDEN = 2816
NORM_EPS = 1e-6
LANES = 128
DT_PAD = LANES

VMEM_LIMIT = 56 * 1024 * 1024


def _cparams(sem):
    return pltpu.CompilerParams(dimension_semantics=sem, vmem_limit_bytes=VMEM_LIMIT)


def _const_spec(arr):
    nd = arr.ndim
    return pl.BlockSpec(arr.shape, lambda *_: (0,) * nd, pipeline_mode=pl.Buffered(1))


def _layer_spec(arr, l):
    nd = arr.ndim - 1
    return pl.BlockSpec((None,) + arr.shape[1:], lambda *_: (l,) + (0,) * nd,
                        pipeline_mode=pl.Buffered(1))


def _mm(a, b):
    return jnp.dot(a.astype(bf16), b.astype(bf16), preferred_element_type=f32)


def _mm_nt(a, b):
    return lax.dot_general(a.astype(bf16), b.astype(bf16), (((1,), (1,)), ((), ())),
                           preferred_element_type=f32)


def _mm_tn(a, b):
    return lax.dot_general(a.astype(bf16), b.astype(bf16), (((0,), (0,)), ((), ())),
                           preferred_element_type=f32)


def _split(x, parts):
    out = []
    for _ in range(parts - 1):
        hi = x.astype(bf16)
        out.append(hi)
        x = x - hi.astype(f32)
    out.append(x.astype(bf16))
    return out


def _mm_exact_rhs(a, b_bf16, parts):
    acc = None
    for piece in _split(a, parts):
        t = jnp.dot(piece, b_bf16, preferred_element_type=f32)
        acc = t if acc is None else acc + t
    return acc


def _mm_exact_lhs(a_bf16, b, parts):
    acc = None
    for piece in _split(b, parts):
        t = jnp.dot(a_bf16, piece, preferred_element_type=f32)
        acc = t if acc is None else acc + t
    return acc


def _rms(x, g, eps):
    return x * lax.rsqrt(jnp.mean(x * x, axis=-1, keepdims=True) + eps) * g


def _sigmoid(x):
    return 1.0 / (1.0 + jnp.exp(-x))


def _softplus(x):
    return jnp.maximum(x, 0.0) + jnp.log(1.0 + jnp.exp(-jnp.abs(x)))


def _silu(x):
    return x * _sigmoid(x)


def _inproj_kernel(x_ref, g_ref, wa_ref, wu_ref, wdt_ref, p_ref, z_ref, xbc_ref, u_ref, dt_ref):
    hb = _rms(x_ref[...], g_ref[...], NORM_EPS).astype(bf16)
    o = 0
    for ref, width in ((p_ref, RW_COLS), (z_ref, SSD_W), (xbc_ref, SSD_XBC)):
        ref[...] = jnp.dot(hb, wa_ref[:, o:o + width], preferred_element_type=f32)
        o += width
    u_ref[...] = jnp.dot(hb, wu_ref[...], preferred_element_type=f32)
    dt_ref[...] = jnp.dot(hb, wdt_ref[...], preferred_element_type=f32)


def _col_block_spec(arr, l, width, idx):
    return pl.BlockSpec((None, arr.shape[1], width), lambda *_: (l, 0, idx), pipeline_mode=pl.Buffered(1))


def _inproj(x, g, w_a, w_ug, w_dt, l, tm):
    B, T, D = x.shape
    nt = T // tm
    row = lambda width: pl.BlockSpec((None, tm, width), lambda b, j: (b, j, 0))
    return pl.pallas_call(
        _inproj_kernel,
        grid=(B, nt),
        in_specs=[row(D), _layer_spec(g, l), _layer_spec(w_a, l), _col_block_spec(w_ug, l, S5_W, 0),
                  _layer_spec(w_dt, l)],
        out_specs=[row(RW_COLS), row(SSD_W), row(SSD_XBC), row(S5_W), row(DT_PAD)],
        out_shape=[jax.ShapeDtypeStruct((B, T, RW_COLS), f32),
                   jax.ShapeDtypeStruct((B, T, SSD_W), f32),
                   jax.ShapeDtypeStruct((B, T, SSD_XBC), f32),
                   jax.ShapeDtypeStruct((B, T, S5_W), f32),
                   jax.ShapeDtypeStruct((B, T, DT_PAD), f32)],
        compiler_params=_cparams(("parallel", "parallel")),
        name="inproj",
    )(x, g, w_a, w_ug, w_dt)


def _stack_heads(x):
    lane = lax.broadcasted_iota(jnp.int32, x.shape, 1)
    lo = lane < RW_N
    return jnp.concatenate([jnp.where(lo, x, 0.0), jnp.where(lo, 0.0, x)], axis=0)


def _head_sums(x, ones_pair):
    R = x.shape[0]
    PW = 2 * RW_N
    st = jnp.concatenate([x[:, q * PW:(q + 1) * PW] for q in range(RW_W // PW)], axis=0)
    s = _mm_exact_rhs(st, ones_pair, 2)
    return jnp.concatenate([s[q * R:(q + 1) * R] for q in range(RW_W // PW)], axis=1)


def _rwkv_kernel(p_ref, mu_ref, w0a0_ref, wwa_ref, gup_ref, kk_ref, ka_ref, rk_ref, lng_ref, lnb_ref,
                 ones_ref, tril_ref, o_ref, prev_ref, h_ref, *, nc):
    C = RW_CHUNK
    TB = nc * C
    PW = 2 * RW_N
    NP = RW_HEADS // 2

    @pl.when(pl.program_id(1) == 0)
    def _():
        prev_ref[...] = jnp.zeros_like(prev_ref)
        h_ref[...] = jnp.zeros_like(h_ref)

    p = p_ref[...]
    row = lax.broadcasted_iota(jnp.int32, p.shape, 0)
    shifted = jnp.where(row == 0, prev_ref[...], pltpu.roll(p, 1, 0))
    prev_ref[...] = p[TB - 1:TB, :]
    pm = p + (shifted - p) * mu_ref[...]
    r = pm[:, 0:RW_W]
    k = pm[:, RW_W:2 * RW_W]
    v = pm[:, 2 * RW_W:3 * RW_W]
    wa_in = pm[:, 3 * RW_W:3 * RW_W + PW]
    gd = pm[:, 3 * RW_W + PW:RW_COLS]
    lane = lax.broadcasted_iota(jnp.int32, wa_in.shape, 1)
    wa_in = jnp.where(lane < RW_DECAY_RANK, jnp.tanh(wa_in), wa_in)
    wa = w0a0_ref[...] + _mm(wa_in, wwa_ref[...])
    w_log = -_softplus(-wa[:, :RW_W]) - 0.5
    lw = -jnp.exp(w_log)
    a = _sigmoid(wa[:, RW_W:])
    g = _mm(_sigmoid(gd), gup_ref[...])

    ones = ones_ref[...]
    kk = k * kk_ref[...]
    kk = kk * lax.rsqrt(jnp.maximum(_head_sums(kk * kk, ones), 1e-24))
    k2 = k * (1.0 + (a - 1.0) * ka_ref[...])

    lc = _mm_exact_lhs(tril_ref[...], lw, 2)
    lc_last = jnp.concatenate(
        [jnp.broadcast_to(lc[(c + 1) * C - 1:(c + 1) * C, :], (C, RW_W)) for c in range(nc)], axis=0)
    e_pos = jnp.exp(lc)
    e_neg = jnp.exp(-lc)
    e_end = jnp.exp(lc_last - lc)
    kka = kk * a
    r_hat = r * e_pos
    a_hat = -kk * jnp.exp(lc - lw)
    b_hat = kka * e_neg
    k_hat = k2 * e_neg
    b_til = kka * e_end
    k_til = k2 * e_end

    ri = lax.broadcasted_iota(jnp.int32, (2 * C, 2 * PW), 0)
    ci = lax.broadcasted_iota(jnp.int32, (2 * C, 2 * PW), 1)
    tri = (ri % C) + jnp.where(ri < C, 0, 1) > (ci % C)
    lane_c = lax.broadcasted_iota(jnp.int32, (C, PW), 1)
    eye_b = lax.broadcasted_iota(jnp.int32, (C, PW), 0) == lane_c % C
    eye_c = jnp.where(eye_b, 1.0, 0.0)

    inst = [(c, q) for c in range(nc) for q in range(NP)]
    every = range(len(inst))

    def pair(t):
        return [t[c * C:(c + 1) * C, q * PW:(q + 1) * PW] for c, q in inst]

    ac, rc = pair(a_hat), pair(r_hat)
    am, bm, km = map(_stack_heads, ac), map(_stack_heads, pair(b_hat)), map(_stack_heads, pair(k_hat))
    am, bm, km = list(am), list(bm), list(km)
    btm, ktm, vm = (list(map(_stack_heads, pair(t))) for t in (b_til, k_til, v))
    a_all = [_mm_nt(jnp.concatenate([ac[i], rc[i]], axis=0), jnp.concatenate([bm[i], km[i]], axis=0))
             for i in every]
    a_all = [jnp.where(tri, t, 0.0) for t in a_all]
    pw = [t[:C, :PW] for t in a_all]
    t_inv = [eye_c + t for t in pw]
    pw = [_mm(t, _stack_heads(t)) for t in pw]
    for _ in range(4):
        both = [_mm(jnp.concatenate([pw[i], t_inv[i]], axis=0), _stack_heads(pw[i])) for i in every]
        t_inv = [t_inv[i] + both[i][C:] for i in every]
        pw = [t[:C] for t in both]
    t_inv = [t_inv[i] + _mm(t_inv[i], _stack_heads(pw[i])) for i in every]
    av = [_mm(a_all[i][:, PW:], vm[i]) for i in every]
    pq = [_mm(t_inv[i], jnp.concatenate([am[i], _stack_heads(av[i][:C])], axis=1)) for i in every]
    pqm = [jnp.concatenate([_stack_heads(t[:, :PW]), _stack_heads(t[:, PW:])], axis=1) for t in pq]
    xy = [_mm(a_all[i][C:, :PW], pqm[i]) for i in every]
    zero = jnp.zeros((PW, PW), f32)
    mkgk = [_mm_tn(jnp.concatenate([btm[i], ktm[i]], axis=0),
                   jnp.concatenate([pqm[i], jnp.concatenate([zero, vm[i]], axis=1)], axis=0)) for i in every]
    mk = [t[:RW_N, :PW] + t[RW_N:, :PW] for t in mkgk]
    gk = [t[:RW_N, PW:] + t[RW_N:, PW:] for t in mkgk]
    mx = [jnp.concatenate([mk[i], rc[i] + xy[i][:, :PW]], axis=0) for i in every]
    ym = [xy[i][:, PW:] + av[i][C:] for i in every]

    def decay_rows(c, q):
        gam = jnp.exp(lc[(c + 1) * C - 1:(c + 1) * C, q * PW:(q + 1) * PW])
        diag = jnp.where(eye_b, gam, 0.0)
        lo = jnp.sum(diag[:, :RW_N], axis=1, keepdims=True)
        hi = jnp.sum(diag[:, RW_N:], axis=1, keepdims=True)
        return jnp.where(lane_c < RW_N, lo, hi)

    gam_c = [decay_rows(c, q) for c, q in inst]

    hs = [h_ref[q] for q in range(NP)]
    o_rows = []
    for c in range(nc):
        seq = [_mm(mx[c * NP + q], _stack_heads(hs[q])) for q in range(NP)]
        hs = [gam_c[c * NP + q] * hs[q] + seq[q][:RW_N] + gk[c * NP + q] for q in range(NP)]
        o_rows.append(jnp.concatenate([seq[q][RW_N:] + ym[c * NP + q] for q in range(NP)], axis=1))
    for q in range(NP):
        h_ref[q] = hs[q]
    o = jnp.concatenate(o_rows, axis=0)

    inv_n = 1.0 / RW_N
    mean = _head_sums(o, ones) * inv_n
    oc = o - mean
    var = _head_sums(oc * oc, ones) * inv_n
    o = oc * lax.rsqrt(var + RW_LN_EPS) * lng_ref[...] + lnb_ref[...]
    bonus = _head_sums(r * k2 * rk_ref[...], ones) * v
    o_ref[...] = (o + bonus) * g


def _rwkv(p, layer_consts, shared_consts, l, nc):
    assert RW_CHUNK == RW_N
    B, T, _ = p.shape
    TB = nc * RW_CHUNK
    return pl.pallas_call(
        functools.partial(_rwkv_kernel, nc=nc),
        grid=(B, T // TB),
        in_specs=[pl.BlockSpec((None, TB, RW_COLS), lambda b, j: (b, j, 0))]
                 + [_layer_spec(c, l) for c in layer_consts] + [_const_spec(c) for c in shared_consts],
        out_specs=pl.BlockSpec((None, TB, RW_W), lambda b, j: (b, j, 0)),
        out_shape=jax.ShapeDtypeStruct((B, T, RW_W), f32),
        scratch_shapes=[pltpu.VMEM((1, RW_COLS), f32),
                        pltpu.VMEM((RW_HEADS // 2, RW_N, 2 * RW_N), f32)],
        compiler_params=_cparams(("parallel", "arbitrary")),
        name="rwkv7",
    )(p, *layer_consts, *shared_consts)


def _ssd_stages(z_ref, xbc_ref, dt_ref, cw_ref, cb_ref, dtb_ref, alog_ref, dskip_ref, ng_ref,
                tril_ref, hist_ref, st_ref, result, *, nc):
    L = SSD_CHUNK
    TB = nc * L
    HIST = 8

    @pl.when(pl.program_id(1) == 0)
    def _():
        hist_ref[...] = jnp.zeros_like(hist_ref)
        st_ref[...] = jnp.zeros_like(st_ref)

    x_in = xbc_ref[...]
    hist = hist_ref[...]
    row8 = lax.broadcasted_iota(jnp.int32, (HIST, SSD_XBC), 0)
    conv = cb_ref[...] + cw_ref[SSD_CONV - 1:SSD_CONV, :] * x_in
    for back in range(1, SSD_CONV):
        rolled = pltpu.roll(x_in, back, 0)
        head = jnp.where(row8 < back, pltpu.roll(hist, back, 0), rolled[0:HIST])
        shifted = jnp.concatenate([head, rolled[HIST:]], axis=0)
        conv = conv + cw_ref[SSD_CONV - 1 - back:SSD_CONV - back, :] * shifted
    hist_ref[...] = x_in[TB - HIST:TB, :]
    yield
    xbc = _silu(conv)
    xs = xbc[:, :SSD_W]
    bmat = xbc[:, SSD_W:SSD_W + SSD_GROUPS * SSD_STATE]
    cmat = xbc[:, SSD_W + SSD_GROUPS * SSD_STATE:]
    yield

    dt = _softplus(dt_ref[...] + dtb_ref[...])
    da = dt * -jnp.exp(alog_ref[...])
    acs = _mm_exact_lhs(tril_ref[...], da, 2)
    yield
    dt_rep = jnp.concatenate([jnp.broadcast_to(dt[:, h:h + 1], (TB, LANES)) for h in range(SSD_HEADS)], axis=1)
    acs_rep = jnp.concatenate([jnp.broadcast_to(acs[:, h:h + 1], (TB, LANES)) for h in range(SSD_HEADS)], axis=1)

    lane = lax.broadcasted_iota(jnp.int32, (TB, LANES), 1)
    lo = lane < SSD_P

    def pair64(rep, q):
        return jnp.where(lo, rep[:, (2 * q) * LANES:(2 * q + 1) * LANES],
                         rep[:, (2 * q + 1) * LANES:(2 * q + 2) * LANES])

    dt64 = jnp.concatenate([pair64(dt_rep, q) for q in range(SSD_HEADS // 2)], axis=1)
    acs64 = jnp.concatenate([pair64(acs_rep, q) for q in range(SSD_HEADS // 2)], axis=1)
    xdt = xs * dt64
    last = [acs64[(c + 1) * L - 1:(c + 1) * L, :] for c in range(nc)]
    acs_last = jnp.concatenate([jnp.broadcast_to(t, (L, SSD_W)) for t in last], axis=0)
    xdec = xdt * jnp.exp(acs_last - acs64)
    e_acs = jnp.exp(acs64)
    chunk_decay = [jnp.exp(t) for t in last]
    yield

    li = lax.broadcasted_iota(jnp.int32, (L, L), 0)
    si = lax.broadcasted_iota(jnp.int32, (L, L), 1)
    causal = li >= si
    eye = li == si
    glane = lax.broadcasted_iota(jnp.int32, (L, SSD_GW), 1)
    hpg = SSD_HEADS // SSD_GROUPS
    G = SSD_GROUPS
    inst = [(c, gi) for c in range(nc) for gi in range(G)]
    every = range(len(inst))
    rows = lambda c: slice(c * L, (c + 1) * L)
    gsl = lambda gi: slice(gi * SSD_GW, (gi + 1) * SSD_GW)
    ssl = lambda gi: slice(gi * SSD_STATE, (gi + 1) * SSD_STATE)
    bg = [bmat[rows(c), ssl(gi)] for c, gi in inst]
    cg = [cmat[rows(c), ssl(gi)] for c, gi in inst]
    scores = [_mm_nt(cg[i], bg[i]) for i in every]
    yield
    wcat, xst = [], []
    for i, (c, gi) in enumerate(inst):
        if i == len(inst) // 2:
            yield
        xg = xdt[rows(c), gsl(gi)]
        wm, xm = [], []
        for j in range(hpg):
            hd = gi * hpg + j
            col = acs_rep[rows(c), hd * LANES:(hd + 1) * LANES]
            rowv = jnp.sum(jnp.where(eye, col, 0.0), axis=0, keepdims=True)
            wm.append(jnp.where(causal, scores[i] * jnp.exp(col - rowv), 0.0))
            xm.append(jnp.where((glane // SSD_P) == j, xg, 0.0))
        wcat.append(jnp.concatenate(wm, axis=1))
        xst.append(jnp.concatenate(xm, axis=0))
    yield
    y_diag = [_mm(wcat[i], xst[i]) for i in every]
    upd = [_mm_tn(bg[i], xdec[rows(c), gsl(gi)]) for i, (c, gi) in enumerate(inst)]
    yield
    st = [st_ref[gi] for gi in range(G)]
    y_rows = []
    for c in range(nc):
        y_off = [_mm(cg[c * G + gi], st[gi]) * e_acs[rows(c), gsl(gi)] for gi in range(G)]
        st = [chunk_decay[c][:, gsl(gi)] * st[gi] + upd[c * G + gi] for gi in range(G)]
        y_rows.append(jnp.concatenate([y_diag[c * G + gi] + y_off[gi] for gi in range(G)], axis=1))
    for gi in range(G):
        st_ref[gi] = st[gi]
    yield
    y = jnp.concatenate(y_rows, axis=0) + xs * dskip_ref[...]
    y = y * _silu(z_ref[...])
    outs = []
    for gi in range(SSD_GROUPS):
        yg = y[:, gi * SSD_GW:(gi + 1) * SSD_GW]
        outs.append(yg * lax.rsqrt(jnp.mean(yg * yg, axis=-1, keepdims=True) + SSD_NORM_EPS))
    result.append(jnp.concatenate(outs, axis=1) * ng_ref[...])


def _s5_kernel(u_ref, lre_ref, lim_ref, lstep_ref, wb_ref, cre_ref, cim_ref, d_ref,
               gw_ref, gb_ref, perm_ref, permt_ref, o_ref, sre_ref, sim_ref, bur_ref, bui_ref, par_ref,
               cpr_ref, cpi_ref, *, nb, tt):
    @pl.when(pl.program_id(0) == 0)
    def _():
        sre_ref[...] = jnp.zeros_like(sre_ref)
        sim_ref[...] = jnp.zeros_like(sim_ref)
        lr = lre_ref[...]
        li = lim_ref[...]
        step = jnp.exp(lstep_ref[...])
        mag = jnp.exp(lr * step)
        a_re = mag * jnp.cos(li * step)
        a_im = mag * jnp.sin(li * step)
        den = lr * lr + li * li
        coef_re = ((a_re - 1.0) * lr + a_im * li) / den
        coef_im = (a_im * lr - (a_re - 1.0) * li) / den
        par_ref[0:1, :] = a_re
        par_ref[1:2, :] = a_im
        ei = lax.broadcasted_iota(jnp.int32, (LANES, LANES), 0)
        ej = lax.broadcasted_iota(jnp.int32, (LANES, LANES), 1)
        eye = ei == ej
        for j in range(S5_SN // LANES):
            ls = slice(j * LANES, (j + 1) * LANES)
            kr = jnp.sum(jnp.where(eye, coef_re[:, ls], 0.0), axis=1, keepdims=True)
            ki = jnp.sum(jnp.where(eye, coef_im[:, ls], 0.0), axis=1, keepdims=True)
            c_re = cre_ref[ls, :]
            c_im = cim_ref[ls, :]
            cpr_ref[ls, :] = (c_re * kr - c_im * ki).astype(bf16)
            cpi_ref[ls, :] = (c_re * ki + c_im * kr).astype(bf16)

    ab_re = par_ref[0:1, :]
    ab_im = par_ref[1:2, :]

    ub = jnp.dot(perm_ref[...], u_ref[...].reshape(nb * tt, S5_W).astype(bf16),
                 preferred_element_type=f32).astype(bf16)
    u = ub.astype(f32)
    LB = LANES // S5_GROUP * S5_STATE
    NBLK = S5_SN // LB
    in_per_tile = LANES // S5_STATE * S5_GROUP
    ys = [None] * NBLK

    def project_in(jb):
        for j in range(jb * LB // LANES, (jb + 1) * LB // LANES):
            us = slice(j * in_per_tile // LANES * LANES, (j * in_per_tile // LANES + 1) * LANES)
            ls = slice(j * LANES, (j + 1) * LANES)
            xri = jnp.dot(ub[:, us], wb_ref[us, 2 * j * LANES:2 * (j + 1) * LANES], preferred_element_type=f32)
            bur_ref[:, ls] = xri[:, :LANES]
            bui_ref[:, ls] = xri[:, LANES:]
            yield

    def project_out(m):
        ks = slice(m * LB, (m + 1) * LB)
        os_ = slice(m * LANES, (m + 1) * LANES)
        y_re = jnp.dot(bur_ref[:, ks].astype(bf16), cpr_ref[ks, os_], preferred_element_type=f32)
        yield
        ys[m] = y_re - jnp.dot(bui_ref[:, ks].astype(bf16), cpi_ref[ks, os_], preferred_element_type=f32)
        yield

    def chain(*gens):
        for gen in gens:
            yield from gen

    for _ in project_in(0):
        pass
    for jb in range(NBLK):
        side = chain(project_in(jb + 1) if jb + 1 < NBLK else (), project_out(jb - 1) if jb > 0 else ())
        ls = slice(jb * LB, (jb + 1) * LB)
        ar = jnp.broadcast_to(ab_re[:, ls], (nb, LB))
        ai = jnp.broadcast_to(ab_im[:, ls], (nb, LB))
        s_re = sre_ref[:, ls]
        s_im = sim_ref[:, ls]
        for t in range(tt):
            rows = slice(t * nb, (t + 1) * nb)
            s_re, s_im = (ar * s_re - ai * s_im + bur_ref[rows, ls], ar * s_im + ai * s_re + bui_ref[rows, ls])
            bur_ref[rows, ls] = s_re
            bui_ref[rows, ls] = s_im
            if t % 5 == 4:
                next(side, None)
        sre_ref[:, ls] = s_re
        sim_ref[:, ls] = s_im
        for _ in side:
            pass
    for _ in project_out(NBLK - 1):
        pass
    y = jnp.concatenate(ys, axis=1) + d_ref[...] * u
    zg = 0.5 * y * (1.0 + jnp.tanh(0.7978845608028654 * (y + 0.044715 * (y * y * y))))
    out = (zg * _sigmoid(_mm(zg, gw_ref[...]) + gb_ref[...])).astype(bf16)
    o_ref[...] = jnp.dot(permt_ref[...], out, preferred_element_type=f32).astype(bf16).reshape(nb, tt, S5_W)


def _s5(u, layer_consts, l, tt):
    B, T, _ = u.shape
    blk = tt * B
    src = jnp.arange(blk)
    perm = (((src % B) * tt + src // B)[:, None] == src[None, :]).astype(bf16)
    shared = (perm, perm.T)
    spec = pl.BlockSpec((B, tt, S5_W), lambda i: (0, i, 0))
    return pl.pallas_call(
        functools.partial(_s5_kernel, nb=B, tt=tt),
        grid=(T // tt,),
        in_specs=[spec] + [_layer_spec(c, l) for c in layer_consts] + [_const_spec(c) for c in shared],
        out_specs=spec,
        out_shape=jax.ShapeDtypeStruct((B, T, S5_W), bf16),
        scratch_shapes=[pltpu.VMEM((B, S5_SN), f32), pltpu.VMEM((B, S5_SN), f32),
                        pltpu.VMEM((blk, S5_SN), f32), pltpu.VMEM((blk, S5_SN), f32),
                        pltpu.VMEM((8, S5_SN), f32),
                        pltpu.VMEM((S5_SN, S5_W), bf16), pltpu.VMEM((S5_SN, S5_W), bf16)],
        compiler_params=_cparams(("arbitrary",)),
        name="s5",
    )(u, *layer_consts, *shared)


GATE_BLOCKS = N_BRANCHES * D_MODEL // S5_W


N_MERGE_W = 4


def _merge_kernel(x_ref, oa_ref, oc_ref, z_ref, xbc_ref, dt_ref, g_ref, *rest, nc):
    wg_refs = rest[:GATE_BLOCKS]
    pa_ref, pb_ref, pc_ref, wo_ref = rest[GATE_BLOCKS:GATE_BLOCKS + N_MERGE_W]
    ssd_refs = rest[GATE_BLOCKS + N_MERGE_W:-3]
    out_ref, hist_ref, st_ref = rest[-3:]
    per = GATE_BLOCKS // N_BRANCHES
    ssd_out = []
    ssd = _ssd_stages(z_ref, xbc_ref, dt_ref, *ssd_refs, hist_ref, st_ref, ssd_out, nc=nc)
    ya = jnp.dot(oa_ref[...].astype(bf16), pa_ref[...], preferred_element_type=f32)
    next(ssd)
    x = x_ref[...]
    hb = _rms(x, g_ref[...], NORM_EPS).astype(bf16)
    yc = jnp.dot(oc_ref[...].astype(bf16), pc_ref[...], preferred_element_type=f32)
    next(ssd, None)
    logit_blocks = []
    for w in wg_refs:
        logit_blocks.append(jnp.dot(hb, w[...], preferred_element_type=f32))
        next(ssd, None)
    gates = [_sigmoid(jnp.concatenate(logit_blocks[i * per:(i + 1) * per], axis=1)) for i in range(N_BRANCHES)]
    for _ in ssd:
        pass
    merged = gates[0] * ya + gates[2] * yc
    yb = jnp.dot(ssd_out[0].astype(bf16), pb_ref[...], preferred_element_type=f32)
    merged = merged + gates[1] * yb
    out_ref[...] = x + jnp.dot(merged.astype(bf16), wo_ref[...], preferred_element_type=f32)


def _merge(x, oa, oc, z, xbc, dt, g, w_ug, layer_consts, ssd_layer, ssd_shared, l, tm):
    B, T, D = x.shape
    nc = tm // SSD_CHUNK
    row = lambda width: pl.BlockSpec((None, tm, width), lambda b, j: (b, j, 0))
    gate_specs = [_col_block_spec(w_ug, l, S5_W, 1 + k) for k in range(GATE_BLOCKS)]
    return pl.pallas_call(
        functools.partial(_merge_kernel, nc=nc),
        grid=(B, T // tm),
        in_specs=[row(D), row(RW_W), row(S5_W), row(SSD_W), row(SSD_XBC), row(DT_PAD), _layer_spec(g, l)]
                 + gate_specs + [_layer_spec(c, l) for c in layer_consts]
                 + [_layer_spec(c, l) for c in ssd_layer] + [_const_spec(c) for c in ssd_shared],
        out_specs=row(D),
        out_shape=jax.ShapeDtypeStruct((B, T, D), f32),
        scratch_shapes=[pltpu.VMEM((8, SSD_XBC), f32),
                        pltpu.VMEM((SSD_GROUPS, SSD_STATE, SSD_GW), f32)],
        compiler_params=_cparams(("parallel", "arbitrary")),
        name="merge_ssd",
    )(x, oa, oc, z, xbc, dt, g, *([w_ug] * GATE_BLOCKS), *layer_consts, *ssd_layer, *ssd_shared)


def _ffn_kernel(x_ref, g_ref, w1_ref, w2_ref, fg_ref, out_ref, *, final):
    x = x_ref[...]
    hb = _rms(x, g_ref[...], NORM_EPS).astype(bf16)
    gate = jnp.dot(hb, w1_ref[:, :FFN_HIDDEN], preferred_element_type=f32)
    up = jnp.dot(hb, w1_ref[:, FFN_HIDDEN:], preferred_element_type=f32)
    act = (_silu(gate) * up).astype(bf16)
    y = x + jnp.dot(act, w2_ref[...], preferred_element_type=f32)
    if final:
        y = _rms(y, fg_ref[...], NORM_EPS)
    out_ref[...] = y


def _ffn(x, layer_consts, fg, l, tm, final):
    B, T, D = x.shape
    row = pl.BlockSpec((None, tm, D), lambda b, j: (b, j, 0))
    return pl.pallas_call(
        functools.partial(_ffn_kernel, final=final),
        grid=(B, T // tm),
        in_specs=[row] + [_layer_spec(c, l) for c in layer_consts] + [_const_spec(fg)],
        out_specs=row,
        out_shape=jax.ShapeDtypeStruct((B, T, D), f32),
        compiler_params=_cparams(("parallel", "parallel")),
        name="ffn",
    )(x, *layer_consts, fg)


W_IN_A = RW_COLS + SSD_W + SSD_XBC
W_IN_UG = W_IN_A + SSD_HEADS
W_IN_COLS = W_IN_UG + S5_W + N_BRANCHES * D_MODEL


def _win_split_kernel(wt_ref, wa_ref, wug_ref, wdt_ref):
    wt = wt_ref[...]
    wa_ref[...] = wt[:W_IN_A, :].T.astype(bf16)
    wug_ref[...] = wt[W_IN_UG:, :].T.astype(bf16)
    dt_rows = jnp.concatenate([wt[W_IN_A:W_IN_UG, :],
                               jnp.zeros((DT_PAD - SSD_HEADS, wt.shape[1]), f32)], axis=0)
    wdt_ref[...] = dt_rows.T.astype(bf16)


def _win_split(w_in, rb):
    depth, D, cols = w_in.shape
    w_t = jnp.swapaxes(w_in, 1, 2)
    spec = lambda width: pl.BlockSpec((None, rb, width), lambda l, i: (l, i, 0))
    return pl.pallas_call(
        _win_split_kernel,
        grid=(depth, D // rb),
        in_specs=[pl.BlockSpec((None, cols, rb), lambda l, i: (l, 0, i))],
        out_specs=[spec(W_IN_A), spec(cols - W_IN_UG), spec(DT_PAD)],
        out_shape=[jax.ShapeDtypeStruct((depth, D, W_IN_A), bf16),
                   jax.ShapeDtypeStruct((depth, D, cols - W_IN_UG), bf16),
                   jax.ShapeDtypeStruct((depth, D, DT_PAD), bf16)],
        compiler_params=_cparams(("parallel", "parallel")),
        name="w_in_split",
    )(w_t)


def _rows(v):
    return v.reshape(v.shape[0], 1, -1).astype(f32)


def _s5_block_diag(w2d, row_group, col_group, dtype):
    tiled = jnp.tile(w2d, (1, 1, S5_GROUPS))
    r = jnp.arange(tiled.shape[1]) // row_group
    c = jnp.arange(tiled.shape[2]) // col_group
    return jnp.where((r[:, None] == c[None, :])[None], tiled, 0.0).astype(dtype)


def _tile_tm(T):
    for tm in (512, 256, 128):
        if T % tm == 0:
            return tm
    raise ValueError("sequence length must be a multiple of 128")


def kernel(x, norm_mix, w_in, rwkv_mu, rwkv_w0, rwkv_w_up, rwkv_a0, rwkv_a_up, rwkv_g_up, rwkv_k_k, rwkv_k_a, rwkv_r_k, rwkv_ln_g, rwkv_ln_b, proj_a, ssd_conv_w, ssd_conv_b, ssd_dt_bias, ssd_a_log, ssd_d, ssd_norm_g, proj_b, s5_lam_re, s5_lam_im, s5_log_step, s5_b_re, s5_b_im, s5_c_re, s5_c_im, s5_d, s5_glu_w, s5_glu_b, proj_c, w_out, norm_ffn, ffn_w_in, ffn_w_out, final_norm):
    B, T, D = x.shape
    depth = w_in.shape[0]
    rw_nc = RW_CHUNKS_PER_STEP
    assert D == D_MODEL and T % SSD_CHUNK == 0 and B % 8 == 0 and T % (rw_nc * RW_CHUNK) == 0
    tm = _tile_tm(T)
    hid = jnp.arange(2 * RW_N) // RW_N
    head_ones = (hid[:, None] == hid[None, :]).astype(bf16)
    c_idx = jnp.arange(rw_nc * RW_CHUNK)
    tril_rw = ((c_idx[:, None] >= c_idx[None, :])
               & (c_idx[:, None] // RW_CHUNK == c_idx[None, :] // RW_CHUNK)).astype(bf16)
    l_idx = jnp.arange(tm)
    tril_ssd = ((l_idx[:, None] >= l_idx[None, :])
                & (l_idx[:, None] // SSD_CHUNK == l_idx[None, :] // SSD_CHUNK)).astype(bf16)

    g_mix = _rows(norm_mix)
    assert w_in.shape[2] == W_IN_COLS
    w_a, w_ug, w_dt = _win_split(w_in, 256)
    zr = jnp.zeros((depth, RW_DECAY_RANK, RW_W), f32)
    w_wa = jnp.concatenate([jnp.concatenate([rwkv_w_up, zr], axis=2),
                            jnp.concatenate([zr, rwkv_a_up], axis=2)], axis=1).astype(bf16)
    rw_layer = (_rows(rwkv_mu), jnp.concatenate([_rows(rwkv_w0), _rows(rwkv_a0)], axis=2), w_wa,
                rwkv_g_up.astype(bf16), _rows(rwkv_k_k), _rows(rwkv_k_a), _rows(rwkv_r_k),
                _rows(rwkv_ln_g), _rows(rwkv_ln_b))
    ssd_layer = (
        jnp.concatenate([ssd_conv_w, jnp.zeros((depth, 8 - SSD_CONV, SSD_XBC), f32)], axis=1),
        _rows(ssd_conv_b),
        jnp.concatenate([_rows(ssd_dt_bias), jnp.zeros((depth, 1, DT_PAD - SSD_HEADS), f32)], axis=2),
        jnp.concatenate([_rows(ssd_a_log), jnp.zeros((depth, 1, DT_PAD - SSD_HEADS), f32)], axis=2),
        _rows(jnp.repeat(ssd_d, SSD_P, axis=1)),
        _rows(ssd_norm_g))
    b_rows = lambda b: jnp.swapaxes(b, 2, 3).reshape(depth, S5_W, S5_STATE)
    c_rows = lambda c: jnp.swapaxes(c, 2, 3).reshape(depth, S5_SN, S5_GROUP)
    nlt = S5_SN // LANES
    b_tiles = lambda b: _s5_block_diag(b_rows(b), S5_GROUP, S5_STATE, bf16).reshape(depth, S5_W, nlt, LANES)
    w_b = jnp.stack([b_tiles(s5_b_re), b_tiles(s5_b_im)],
                    axis=3).reshape(depth, S5_W, 2 * S5_SN)
    s5_layer = (
        _rows(s5_lam_re), _rows(s5_lam_im), _rows(jnp.repeat(s5_log_step, S5_STATE, axis=1)),
        w_b,
        _s5_block_diag(c_rows(s5_c_re), S5_STATE, S5_GROUP, f32),
        _s5_block_diag(c_rows(s5_c_im), S5_STATE, S5_GROUP, f32),
        _rows(s5_d), s5_glu_w.astype(bf16), _rows(s5_glu_b))
    merge_layer = (proj_a.astype(bf16), proj_b.astype(bf16), proj_c.astype(bf16), w_out.astype(bf16))
    ffn_layer = (_rows(norm_ffn), ffn_w_in.astype(bf16), ffn_w_out.astype(bf16))
    fg = final_norm.reshape(1, D).astype(f32)
    s5_tt = 32 if T % 32 == 0 else 8

    for l in range(depth):
        p, z, xbc, u, dt = _inproj(x, g_mix, w_a, w_ug, w_dt, l, tm)
        oa = _rwkv(p, rw_layer, (head_ones, tril_rw), l, rw_nc)
        oc = _s5(u, s5_layer, l, s5_tt)
        x = _merge(x, oa, oc, z, xbc, dt, g_mix, w_ug, merge_layer, ssd_layer, (tril_ssd,), l, tm)
        x = _ffn(x, ffn_layer, fg, l, tm, final=(l == depth - 1))
    return x
```

```python
import functools

import jax
import jax.numpy as jnp
from jax import lax
from jax.experimental import pallas as pl
from jax.experimental.pallas import tpu as pltpu

f32 = jnp.float32
bf16 = jnp.bfloat16

D_MODEL = 1024
RW_HEADS = 8
RW_N = 64
RW_W = RW_HEADS * RW_N
RW_DECAY_RANK = 64
RW_ICLR_RANK = 64
RW_GATE_RANK = 128
RW_COLS = 3 * RW_W + RW_DECAY_RANK + RW_ICLR_RANK + RW_GATE_RANK
RW_LN_EPS = 64e-5
RW_CHUNK = 64
RW_CHUNKS_PER_STEP = 4
SSD_HEADS = 8
SSD_P = 64
SSD_W = SSD_HEADS * SSD_P
SSD_GROUPS = 2
SSD_STATE = 128
SSD_CONV = 4
SSD_CHUNK = 128
SSD_XBC = SSD_W + 2 * SSD_GROUPS * SSD_STATE
SSD_NORM_EPS = 1e-5
SSD_GW = SSD_W // SSD_GROUPS
S5_GROUP = 16
S5_GROUPS = 32
S5_W = S5_GROUP * S5_GROUPS
S5_STATE = 64
S5_SN = S5_GROUPS * S5_STATE
N_BRANCHES = 3
FFN_HIDDEN = 2816
NORM_EPS = 1e-6
LANES = 128
DT_PAD = LANES

VMEM_LIMIT = 56 * 1024 * 1024


def _cparams(sem):
    return pltpu.CompilerParams(dimension_semantics=sem, vmem_limit_bytes=VMEM_LIMIT)


def _const_spec(arr):
    nd = arr.ndim
    return pl.BlockSpec(arr.shape, lambda *_: (0,) * nd, pipeline_mode=pl.Buffered(1))


def _layer_spec(arr, l):
    nd = arr.ndim - 1
    return pl.BlockSpec((None,) + arr.shape[1:], lambda *_: (l,) + (0,) * nd,
                        pipeline_mode=pl.Buffered(1))


def _mm(a, b):
    return jnp.dot(a.astype(bf16), b.astype(bf16), preferred_element_type=f32)


def _mm_nt(a, b):
    return lax.dot_general(a.astype(bf16), b.astype(bf16), (((1,), (1,)), ((), ())),
                           preferred_element_type=f32)


def _mm_tn(a, b):
    return lax.dot_general(a.astype(bf16), b.astype(bf16), (((0,), (0,)), ((), ())),
                           preferred_element_type=f32)


def _split(x, parts):
    out = []
    for _ in range(parts - 1):
        hi = x.astype(bf16)
        out.append(hi)
        x = x - hi.astype(f32)
    out.append(x.astype(bf16))
    return out


def _mm_exact_lhs(a_bf16, b, parts):
    acc = None
    for piece in _split(b, parts):
        t = jnp.dot(a_bf16, piece, preferred_element_type=f32)
        acc = t if acc is None else acc + t
    return acc


def _rms(x, g, eps):
    return x * lax.rsqrt(jnp.mean(x * x, axis=-1, keepdims=True) + eps) * g


def _sigmoid(x):
    return 1.0 / (1.0 + jnp.exp(-x))


def _softplus(x):
    return jnp.maximum(x, 0.0) + jnp.log(1.0 + jnp.exp(-jnp.abs(x)))


def _silu(x):
    return x * _sigmoid(x)


def _inproj_kernel(x_ref, g_ref, wa_ref, wu_ref, wdt_ref, p_ref, z_ref, xbc_ref, u_ref, dt_ref):
    hb = _rms(x_ref[...], g_ref[...], NORM_EPS).astype(bf16)
    o = 0
    for ref, width in ((p_ref, RW_COLS), (z_ref, SSD_W), (xbc_ref, SSD_XBC)):
        ref[...] = jnp.dot(hb, wa_ref[:, o:o + width], preferred_element_type=f32)
        o += width
    u_ref[...] = jnp.dot(hb, wu_ref[...], preferred_element_type=f32)
    dt_ref[...] = jnp.dot(hb, wdt_ref[...], preferred_element_type=f32)


def _col_block_spec(arr, l, width, idx):
    return pl.BlockSpec((None, arr.shape[1], width), lambda *_: (l, 0, idx), pipeline_mode=pl.Buffered(1))


def _inproj(x, g, w_a, w_ug, w_dt, l, tm):
    B, T, D = x.shape
    nt = T // tm
    row = lambda width: pl.BlockSpec((None, tm, width), lambda b, j: (b, j, 0))
    return pl.pallas_call(
        _inproj_kernel,
        grid=(B, nt),
        in_specs=[row(D), _layer_spec(g, l), _layer_spec(w_a, l), _col_block_spec(w_ug, l, S5_W, 0),
                  _layer_spec(w_dt, l)],
        out_specs=[row(RW_COLS), row(SSD_W), row(SSD_XBC), row(S5_W), row(DT_PAD)],
        out_shape=[jax.ShapeDtypeStruct((B, T, RW_COLS), f32),
                   jax.ShapeDtypeStruct((B, T, SSD_W), f32),
                   jax.ShapeDtypeStruct((B, T, SSD_XBC), f32),
                   jax.ShapeDtypeStruct((B, T, S5_W), f32),
                   jax.ShapeDtypeStruct((B, T, DT_PAD), f32)],
        compiler_params=_cparams(("parallel", "parallel")),
        name="inproj",
    )(x, g, w_a, w_ug, w_dt)


def _stack_heads(x):
    lane = lax.broadcasted_iota(jnp.int32, x.shape, 1)
    lo = lane < RW_N
    return jnp.concatenate([jnp.where(lo, x, 0.0), jnp.where(lo, 0.0, x)], axis=0)


def _head_sums(x, ones_pair):
    R = x.shape[0]
    PW = 2 * RW_N
    st = jnp.concatenate([x[:, q * PW:(q + 1) * PW] for q in range(RW_W // PW)], axis=0)
    s = _mm(st, ones_pair)
    return jnp.concatenate([s[q * R:(q + 1) * R] for q in range(RW_W // PW)], axis=1)


def _rwkv_kernel(p_ref, mu_ref, w0a0_ref, wwa_ref, gup_ref, kk_ref, ka_ref, rk_ref, lng_ref, lnb_ref,
                 ones_ref, tril_ref, o_ref, prev_ref, h_ref, *, nc):
    C = RW_CHUNK
    TB = nc * C
    PW = 2 * RW_N
    NP = RW_HEADS // 2

    @pl.when(pl.program_id(1) == 0)
    def _():
        prev_ref[...] = jnp.zeros_like(prev_ref)
        h_ref[...] = jnp.zeros_like(h_ref)

    p = p_ref[...]
    row = lax.broadcasted_iota(jnp.int32, p.shape, 0)
    shifted = jnp.where(row == 0, prev_ref[...], pltpu.roll(p, 1, 0))
    prev_ref[...] = p[TB - 1:TB, :]
    pm = p + (shifted - p) * mu_ref[...]
    r = pm[:, 0:RW_W]
    k = pm[:, RW_W:2 * RW_W]
    v = pm[:, 2 * RW_W:3 * RW_W]
    wa_in = pm[:, 3 * RW_W:3 * RW_W + PW]
    gd = pm[:, 3 * RW_W + PW:RW_COLS]
    lane = lax.broadcasted_iota(jnp.int32, wa_in.shape, 1)
    wa_in = jnp.where(lane < RW_DECAY_RANK, jnp.tanh(wa_in), wa_in)
    wa = w0a0_ref[...] + _mm(wa_in, wwa_ref[...])
    w_log = -_softplus(-wa[:, :RW_W]) - 0.5
    lw = -jnp.exp(w_log)
    a = _sigmoid(wa[:, RW_W:])
    g = _mm(_sigmoid(gd), gup_ref[...])

    ones = ones_ref[...]
    kk = k * kk_ref[...]
    kk = kk * lax.rsqrt(jnp.maximum(_head_sums(kk * kk, ones), 1e-24))
    k2 = k * (1.0 + (a - 1.0) * ka_ref[...])

    lc = _mm_exact_lhs(tril_ref[...], lw, 2)
    lc_last = jnp.concatenate(
        [jnp.broadcast_to(lc[(c + 1) * C - 1:(c + 1) * C, :], (C, RW_W)) for c in range(nc)], axis=0)
    e_pos = jnp.exp(lc)
    e_neg = jnp.exp(-lc)
    e_end = jnp.exp(lc_last - lc)
    kka = kk * a
    r_hat = r * e_pos
    a_hat = -kk * jnp.exp(lc - lw)
    b_hat = kka * e_neg
    k_hat = k2 * e_neg
    b_til = kka * e_end
    k_til = k2 * e_end

    ri = lax.broadcasted_iota(jnp.int32, (2 * C, 2 * PW), 0)
    ci = lax.broadcasted_iota(jnp.int32, (2 * C, 2 * PW), 1)
    tri = (ri % C) + jnp.where(ri < C, 0, 1) > (ci % C)
    lane_c = lax.broadcasted_iota(jnp.int32, (C, PW), 1)
    eye_b = lax.broadcasted_iota(jnp.int32, (C, PW), 0) == lane_c % C
    eye_c = jnp.where(eye_b, 1.0, 0.0)

    inst = [(c, q) for c in range(nc) for q in range(NP)]
    every = range(len(inst))

    def pair(t):
        return [t[c * C:(c + 1) * C, q * PW:(q + 1) * PW] for c, q in inst]

    ac, rc = pair(a_hat), pair(r_hat)
    am, bm, km = map(_stack_heads, ac), map(_stack_heads, pair(b_hat)), map(_stack_heads, pair(k_hat))
    am, bm, km = list(am), list(bm), list(km)
    btm, ktm, vm = (list(map(_stack_heads, pair(t))) for t in (b_til, k_til, v))
    a_all = [_mm_nt(jnp.concatenate([ac[i], rc[i]], axis=0), jnp.concatenate([bm[i], km[i]], axis=0))
             for i in every]
    a_all = [jnp.where(tri, t, 0.0) for t in a_all]
    pw = [t[:C, :PW] for t in a_all]
    t_inv = [eye_c + t for t in pw]
    pw = [_mm(t, _stack_heads(t)) for t in pw]
    for _ in range(4):
        both = [_mm(jnp.concatenate([pw[i], t_inv[i]], axis=0), _stack_heads(pw[i])) for i in every]
        t_inv = [t_inv[i] + both[i][C:] for i in every]
        pw = [t[:C] for t in both]
    t_inv = [t_inv[i] + _mm(t_inv[i], _stack_heads(pw[i])) for i in every]
    av = [_mm(a_all[i][:, PW:], vm[i]) for i in every]
    pq = [_mm(t_inv[i], jnp.concatenate([am[i], _stack_heads(av[i][:C])], axis=1)) for i in every]
    pqm = [jnp.concatenate([_stack_heads(t[:, :PW]), _stack_heads(t[:, PW:])], axis=1) for t in pq]
    xy = [_mm(a_all[i][C:, :PW], pqm[i]) for i in every]
    zero = jnp.zeros((PW, PW), f32)
    mkgk = [_mm_tn(jnp.concatenate([btm[i], ktm[i]], axis=0),
                   jnp.concatenate([pqm[i], jnp.concatenate([zero, vm[i]], axis=1)], axis=0)) for i in every]
    mk = [t[:RW_N, :PW] + t[RW_N:, :PW] for t in mkgk]
    gk = [t[:RW_N, PW:] + t[RW_N:, PW:] for t in mkgk]
    mx = [jnp.concatenate([mk[i], rc[i] + xy[i][:, :PW]], axis=0) for i in every]
    ym = [xy[i][:, PW:] + av[i][C:] for i in every]

    def decay_rows(c, q):
        gam = jnp.exp(lc[(c + 1) * C - 1:(c + 1) * C, q * PW:(q + 1) * PW])
        diag = jnp.where(eye_b, gam, 0.0)
        lo = jnp.sum(diag[:, :RW_N], axis=1, keepdims=True)
        hi = jnp.sum(diag[:, RW_N:], axis=1, keepdims=True)
        return jnp.where(lane_c < RW_N, lo, hi)

    gam_c = [decay_rows(c, q) for c, q in inst]

    hs = [h_ref[q] for q in range(NP)]
    o_rows = []
    for c in range(nc):
        seq = [_mm(mx[c * NP + q], _stack_heads(hs[q])) for q in range(NP)]
        hs = [gam_c[c * NP + q] * hs[q] + seq[q][:RW_N] + gk[c * NP + q] for q in range(NP)]
        o_rows.append(jnp.concatenate([seq[q][RW_N:] + ym[c * NP + q] for q in range(NP)], axis=1))
    for q in range(NP):
        h_ref[q] = hs[q]
    o = jnp.concatenate(o_rows, axis=0)

    inv_n = 1.0 / RW_N
    mean = _head_sums(o, ones) * inv_n
    oc = o - mean
    var = _head_sums(oc * oc, ones) * inv_n
    o = oc * lax.rsqrt(var + RW_LN_EPS) * lng_ref[...] + lnb_ref[...]
    bonus = _head_sums(r * k2 * rk_ref[...], ones) * v
    o_ref[...] = (o + bonus) * g


def _rwkv(p, layer_consts, shared_consts, l, nc):
    assert RW_CHUNK == RW_N
    B, T, _ = p.shape
    TB = nc * RW_CHUNK
    return pl.pallas_call(
        functools.partial(_rwkv_kernel, nc=nc),
        grid=(B, T // TB),
        in_specs=[pl.BlockSpec((None, TB, RW_COLS), lambda b, j: (b, j, 0))]
                 + [_layer_spec(c, l) for c in layer_consts] + [_const_spec(c) for c in shared_consts],
        out_specs=pl.BlockSpec((None, TB, RW_W), lambda b, j: (b, j, 0)),
        out_shape=jax.ShapeDtypeStruct((B, T, RW_W), f32),
        scratch_shapes=[pltpu.VMEM((1, RW_COLS), f32),
                        pltpu.VMEM((RW_HEADS // 2, RW_N, 2 * RW_N), f32)],
        compiler_params=_cparams(("parallel", "arbitrary")),
        name="rwkv7",
    )(p, *layer_consts, *shared_consts)


def _ssd_stages(z_ref, xbc_ref, dt_ref, cw_ref, cb_ref, dtb_ref, alog_ref, dskip_ref, ng_ref,
                tril_ref, hist_ref, st_ref, result, *, nc):
    L = SSD_CHUNK
    TB = nc * L
    HIST = 8

    @pl.when(pl.program_id(1) == 0)
    def _():
        hist_ref[...] = jnp.zeros_like(hist_ref)
        st_ref[...] = jnp.zeros_like(st_ref)

    x_in = xbc_ref[...]
    hist = hist_ref[...]
    row8 = lax.broadcasted_iota(jnp.int32, (HIST, SSD_XBC), 0)
    conv = cb_ref[...] + cw_ref[SSD_CONV - 1:SSD_CONV, :] * x_in
    for back in range(1, SSD_CONV):
        rolled = pltpu.roll(x_in, back, 0)
        head = jnp.where(row8 < back, pltpu.roll(hist, back, 0), rolled[0:HIST])
        shifted = jnp.concatenate([head, rolled[HIST:]], axis=0)
        conv = conv + cw_ref[SSD_CONV - 1 - back:SSD_CONV - back, :] * shifted
    hist_ref[...] = x_in[TB - HIST:TB, :]
    yield
    xbc = _silu(conv)
    xs = xbc[:, :SSD_W]
    bmat = xbc[:, SSD_W:SSD_W + SSD_GROUPS * SSD_STATE]
    cmat = xbc[:, SSD_W + SSD_GROUPS * SSD_STATE:]
    yield

    dt = _softplus(dt_ref[...] + dtb_ref[...])
    da = dt * -jnp.exp(alog_ref[...])
    acs = _mm_exact_lhs(tril_ref[...], da, 2)
    yield
    dt_rep = jnp.concatenate([jnp.broadcast_to(dt[:, h:h + 1], (TB, LANES)) for h in range(SSD_HEADS)], axis=1)
    acs_rep = jnp.concatenate([jnp.broadcast_to(acs[:, h:h + 1], (TB, LANES)) for h in range(SSD_HEADS)], axis=1)

    lane = lax.broadcasted_iota(jnp.int32, (TB, LANES), 1)
    lo = lane < SSD_P

    def pair64(rep, q):
        return jnp.where(lo, rep[:, (2 * q) * LANES:(2 * q + 1) * LANES],
                         rep[:, (2 * q + 1) * LANES:(2 * q + 2) * LANES])

    dt64 = jnp.concatenate([pair64(dt_rep, q) for q in range(SSD_HEADS // 2)], axis=1)
    acs64 = jnp.concatenate([pair64(acs_rep, q) for q in range(SSD_HEADS // 2)], axis=1)
    xdt = xs * dt64
    last = [acs64[(c + 1) * L - 1:(c + 1) * L, :] for c in range(nc)]
    acs_last = jnp.concatenate([jnp.broadcast_to(t, (L, SSD_W)) for t in last], axis=0)
    xdec = xdt * jnp.exp(acs_last - acs64)
    e_acs = jnp.exp(acs64)
    chunk_decay = [jnp.exp(t) for t in last]
    yield

    li = lax.broadcasted_iota(jnp.int32, (L, L), 0)
    si = lax.broadcasted_iota(jnp.int32, (L, L), 1)
    causal = li >= si
    eye = li == si
    glane = lax.broadcasted_iota(jnp.int32, (L, SSD_GW), 1)
    hpg = SSD_HEADS // SSD_GROUPS
    G = SSD_GROUPS
    inst = [(c, gi) for c in range(nc) for gi in range(G)]
    every = range(len(inst))
    rows = lambda c: slice(c * L, (c + 1) * L)
    gsl = lambda gi: slice(gi * SSD_GW, (gi + 1) * SSD_GW)
    ssl = lambda gi: slice(gi * SSD_STATE, (gi + 1) * SSD_STATE)
    bg = [bmat[rows(c), ssl(gi)] for c, gi in inst]
    cg = [cmat[rows(c), ssl(gi)] for c, gi in inst]
    scores = [_mm_nt(cg[i], bg[i]) for i in every]
    yield
    wcat, xst = [], []
    for i, (c, gi) in enumerate(inst):
        if i == len(inst) // 2:
            yield
        xg = xdt[rows(c), gsl(gi)]
        wm, xm = [], []
        for j in range(hpg):
            hd = gi * hpg + j
            col = acs_rep[rows(c), hd * LANES:(hd + 1) * LANES]
            rowv = jnp.sum(jnp.where(eye, col, 0.0), axis=0, keepdims=True)
            wm.append(jnp.where(causal, scores[i] * jnp.exp(col - rowv), 0.0))
            xm.append(jnp.where((glane // SSD_P) == j, xg, 0.0))
        wcat.append(jnp.concatenate(wm, axis=1))
        xst.append(jnp.concatenate(xm, axis=0))
    yield
    y_diag = [_mm(wcat[i], xst[i]) for i in every]
    upd = [_mm_tn(bg[i], xdec[rows(c), gsl(gi)]) for i, (c, gi) in enumerate(inst)]
    yield
    st = [st_ref[gi] for gi in range(G)]
    y_rows = []
    for c in range(nc):
        y_off = [_mm(cg[c * G + gi], st[gi]) * e_acs[rows(c), gsl(gi)] for gi in range(G)]
        st = [chunk_decay[c][:, gsl(gi)] * st[gi] + upd[c * G + gi] for gi in range(G)]
        y_rows.append(jnp.concatenate([y_diag[c * G + gi] + y_off[gi] for gi in range(G)], axis=1))
    for gi in range(G):
        st_ref[gi] = st[gi]
    yield
    y = jnp.concatenate(y_rows, axis=0) + xs * dskip_ref[...]
    y = y * _silu(z_ref[...])
    outs = []
    for gi in range(SSD_GROUPS):
        yg = y[:, gi * SSD_GW:(gi + 1) * SSD_GW]
        outs.append(yg * lax.rsqrt(jnp.mean(yg * yg, axis=-1, keepdims=True) + SSD_NORM_EPS))
    result.append(jnp.concatenate(outs, axis=1) * ng_ref[...])


def _s5_kernel(u_ref, lre_ref, lim_ref, lstep_ref, wb_ref, cre_ref, cim_ref, d_ref,
               gw_ref, gb_ref, perm_ref, permt_ref, o_ref, sre_ref, sim_ref, bur_ref, bui_ref, par_ref,
               cpr_ref, cpi_ref, *, nb, tt):
    @pl.when(pl.program_id(0) == 0)
    def _():
        sre_ref[...] = jnp.zeros_like(sre_ref)
        sim_ref[...] = jnp.zeros_like(sim_ref)
        lr = lre_ref[...]
        li = lim_ref[...]
        step = jnp.exp(lstep_ref[...])
        mag = jnp.exp(lr * step)
        a_re = mag * jnp.cos(li * step)
        a_im = mag * jnp.sin(li * step)
        den = lr * lr + li * li
        coef_re = ((a_re - 1.0) * lr + a_im * li) / den
        coef_im = (a_im * lr - (a_re - 1.0) * li) / den
        par_ref[0:1, :] = a_re
        par_ref[1:2, :] = a_im
        ei = lax.broadcasted_iota(jnp.int32, (LANES, LANES), 0)
        ej = lax.broadcasted_iota(jnp.int32, (LANES, LANES), 1)
        eye = ei == ej
        for j in range(S5_SN // LANES):
            ls = slice(j * LANES, (j + 1) * LANES)
            kr = jnp.sum(jnp.where(eye, coef_re[:, ls], 0.0), axis=1, keepdims=True)
            ki = jnp.sum(jnp.where(eye, coef_im[:, ls], 0.0), axis=1, keepdims=True)
            c_re = cre_ref[ls, :]
            c_im = cim_ref[ls, :]
            cpr_ref[ls, :] = (c_re * kr - c_im * ki).astype(bf16)
            cpi_ref[ls, :] = (c_re * ki + c_im * kr).astype(bf16)

    ab_re = par_ref[0:1, :]
    ab_im = par_ref[1:2, :]

    ub = jnp.dot(perm_ref[...], u_ref[...].reshape(nb * tt, S5_W).astype(bf16),
                 preferred_element_type=f32).astype(bf16)
    u = ub.astype(f32)
    LB = LANES // S5_GROUP * S5_STATE
    NBLK = S5_SN // LB
    in_per_tile = LANES // S5_STATE * S5_GROUP
    ys = [None] * NBLK

    def project_in(jb):
        for j in range(jb * LB // LANES, (jb + 1) * LB // LANES):
            us = slice(j * in_per_tile // LANES * LANES, (j * in_per_tile // LANES + 1) * LANES)
            ls = slice(j * LANES, (j + 1) * LANES)
            xri = jnp.dot(ub[:, us], wb_ref[us, 2 * j * LANES:2 * (j + 1) * LANES], preferred_element_type=f32)
            bur_ref[:, ls] = xri[:, :LANES]
            bui_ref[:, ls] = xri[:, LANES:]
            yield

    def project_out(m):
        ks = slice(m * LB, (m + 1) * LB)
        os_ = slice(m * LANES, (m + 1) * LANES)
        y_re = jnp.dot(bur_ref[:, ks].astype(bf16), cpr_ref[ks, os_], preferred_element_type=f32)
        yield
        ys[m] = y_re - jnp.dot(bui_ref[:, ks].astype(bf16), cpi_ref[ks, os_], preferred_element_type=f32)
        yield

    def chain(*gens):
        for gen in gens:
            yield from gen

    for _ in project_in(0):
        pass
    for jb in range(NBLK):
        side = chain(project_in(jb + 1) if jb + 1 < NBLK else (), project_out(jb - 1) if jb > 0 else ())
        ls = slice(jb * LB, (jb + 1) * LB)
        ar = jnp.broadcast_to(ab_re[:, ls], (nb, LB))
        ai = jnp.broadcast_to(ab_im[:, ls], (nb, LB))
        s_re = sre_ref[:, ls]
        s_im = sim_ref[:, ls]
        for t in range(tt):
            rows = slice(t * nb, (t + 1) * nb)
            s_re, s_im = (ar * s_re - ai * s_im + bur_ref[rows, ls], ar * s_im + ai * s_re + bui_ref[rows, ls])
            bur_ref[rows, ls] = s_re
            bui_ref[rows, ls] = s_im
            if t % 5 == 4:
                next(side, None)
        sre_ref[:, ls] = s_re
        sim_ref[:, ls] = s_im
        for _ in side:
            pass
    for _ in project_out(NBLK - 1):
        pass
    y = jnp.concatenate(ys, axis=1) + d_ref[...] * u
    zg = 0.5 * y * (1.0 + jnp.tanh(0.7978845608028654 * (y + 0.044715 * (y * y * y))))
    out = (zg * _sigmoid(_mm(zg, gw_ref[...]) + gb_ref[...])).astype(bf16)
    o_ref[...] = jnp.dot(permt_ref[...], out, preferred_element_type=f32).astype(bf16).reshape(nb, tt, S5_W)


def _s5(u, layer_consts, l, tt):
    B, T, _ = u.shape
    blk = tt * B
    src = jnp.arange(blk)
    perm = (((src % B) * tt + src // B)[:, None] == src[None, :]).astype(bf16)
    shared = (perm, perm.T)
    spec = pl.BlockSpec((B, tt, S5_W), lambda i: (0, i, 0))
    return pl.pallas_call(
        functools.partial(_s5_kernel, nb=B, tt=tt),
        grid=(T // tt,),
        in_specs=[spec] + [_layer_spec(c, l) for c in layer_consts] + [_const_spec(c) for c in shared],
        out_specs=spec,
        out_shape=jax.ShapeDtypeStruct((B, T, S5_W), bf16),
        scratch_shapes=[pltpu.VMEM((B, S5_SN), f32), pltpu.VMEM((B, S5_SN), f32),
                        pltpu.VMEM((blk, S5_SN), f32), pltpu.VMEM((blk, S5_SN), f32),
                        pltpu.VMEM((8, S5_SN), f32),
                        pltpu.VMEM((S5_SN, S5_W), bf16), pltpu.VMEM((S5_SN, S5_W), bf16)],
        compiler_params=_cparams(("arbitrary",)),
        name="s5",
    )(u, *layer_consts, *shared)


GATE_BLOCKS = N_BRANCHES * D_MODEL // S5_W


N_MERGE_W = 4


def _merge_kernel(x_ref, oa_ref, oc_ref, z_ref, xbc_ref, dt_ref, g_ref, *rest, nc):
    wg_refs = rest[:GATE_BLOCKS]
    pa_ref, pb_ref, pc_ref, wo_ref = rest[GATE_BLOCKS:GATE_BLOCKS + N_MERGE_W]
    ssd_refs = rest[GATE_BLOCKS + N_MERGE_W:-3]
    out_ref, hist_ref, st_ref = rest[-3:]
    per = GATE_BLOCKS // N_BRANCHES
    ssd_out = []
    ssd = _ssd_stages(z_ref, xbc_ref, dt_ref, *ssd_refs, hist_ref, st_ref, ssd_out, nc=nc)
    ya = jnp.dot(oa_ref[...].astype(bf16), pa_ref[...], preferred_element_type=f32)
    next(ssd)
    x = x_ref[...]
    hb = _rms(x, g_ref[...], NORM_EPS).astype(bf16)
    yc = jnp.dot(oc_ref[...].astype(bf16), pc_ref[...], preferred_element_type=f32)
    next(ssd, None)
    logit_blocks = []
    for w in wg_refs:
        logit_blocks.append(jnp.dot(hb, w[...], preferred_element_type=f32))
        next(ssd, None)
    gates = [_sigmoid(jnp.concatenate(logit_blocks[i * per:(i + 1) * per], axis=1)) for i in range(N_BRANCHES)]
    for _ in ssd:
        pass
    merged = gates[0] * ya + gates[2] * yc
    yb = jnp.dot(ssd_out[0].astype(bf16), pb_ref[...], preferred_element_type=f32)
    merged = merged + gates[1] * yb
    out_ref[...] = x + jnp.dot(merged.astype(bf16), wo_ref[...], preferred_element_type=f32)


def _merge(x, oa, oc, z, xbc, dt, g, w_ug, layer_consts, ssd_layer, ssd_shared, l, tm):
    B, T, D = x.shape
    nc = tm // SSD_CHUNK
    row = lambda width: pl.BlockSpec((None, tm, width), lambda b, j: (b, j, 0))
    gate_specs = [_col_block_spec(w_ug, l, S5_W, 1 + k) for k in range(GATE_BLOCKS)]
    return pl.pallas_call(
        functools.partial(_merge_kernel, nc=nc),
        grid=(B, T // tm),
        in_specs=[row(D), row(RW_W), row(S5_W), row(SSD_W), row(SSD_XBC), row(DT_PAD), _layer_spec(g, l)]
                 + gate_specs + [_layer_spec(c, l) for c in layer_consts]
                 + [_layer_spec(c, l) for c in ssd_layer] + [_const_spec(c) for c in ssd_shared],
        out_specs=row(D),
        out_shape=jax.ShapeDtypeStruct((B, T, D), f32),
        scratch_shapes=[pltpu.VMEM((8, SSD_XBC), f32),
                        pltpu.VMEM((SSD_GROUPS, SSD_STATE, SSD_GW), f32)],
        compiler_params=_cparams(("parallel", "arbitrary")),
        name="merge_ssd",
    )(x, oa, oc, z, xbc, dt, g, *([w_ug] * GATE_BLOCKS), *layer_consts, *ssd_layer, *ssd_shared)


def _ffn_kernel(x_ref, g_ref, w1_ref, w2_ref, fg_ref, out_ref, *, final):
    x = x_ref[...]
    hb = _rms(x, g_ref[...], NORM_EPS).astype(bf16)
    gate = jnp.dot(hb, w1_ref[:, :FFN_HIDDEN], preferred_element_type=f32)
    up = jnp.dot(hb, w1_ref[:, FFN_HIDDEN:], preferred_element_type=f32)
    act = (_silu(gate) * up).astype(bf16)
    y = x + jnp.dot(act, w2_ref[...], preferred_element_type=f32)
    if final:
        y = _rms(y, fg_ref[...], NORM_EPS)
    out_ref[...] = y


def _ffn(x, layer_consts, fg, l, tm, final):
    B, T, D = x.shape
    row = pl.BlockSpec((None, tm, D), lambda b, j: (b, j, 0))
    return pl.pallas_call(
        functools.partial(_ffn_kernel, final=final),
        grid=(B, T // tm),
        in_specs=[row] + [_layer_spec(c, l) for c in layer_consts] + [_const_spec(fg)],
        out_specs=row,
        out_shape=jax.ShapeDtypeStruct((B, T, D), f32),
        compiler_params=_cparams(("parallel", "parallel")),
        name="ffn",
    )(x, *layer_consts, fg)


W_IN_A = RW_COLS + SSD_W + SSD_XBC
W_IN_UG = W_IN_A + SSD_HEADS
W_IN_COLS = W_IN_UG + S5_W + N_BRANCHES * D_MODEL


def _win_split_kernel(wt_ref, wa_ref, wug_ref, wdt_ref):
    wt = wt_ref[...]
    wa_ref[...] = wt[:W_IN_A, :].T.astype(bf16)
    wug_ref[...] = wt[W_IN_UG:, :].T.astype(bf16)
    dt_rows = jnp.concatenate([wt[W_IN_A:W_IN_UG, :],
                               jnp.zeros((DT_PAD - SSD_HEADS, wt.shape[1]), f32)], axis=0)
    wdt_ref[...] = dt_rows.T.astype(bf16)


def _win_split(w_in, rb):
    depth, D, cols = w_in.shape
    w_t = jnp.swapaxes(w_in, 1, 2)
    spec = lambda width: pl.BlockSpec((None, rb, width), lambda l, i: (l, i, 0))
    return pl.pallas_call(
        _win_split_kernel,
        grid=(depth, D // rb),
        in_specs=[pl.BlockSpec((None, cols, rb), lambda l, i: (l, 0, i))],
        out_specs=[spec(W_IN_A), spec(cols - W_IN_UG), spec(DT_PAD)],
        out_shape=[jax.ShapeDtypeStruct((depth, D, W_IN_A), bf16),
                   jax.ShapeDtypeStruct((depth, D, cols - W_IN_UG), bf16),
                   jax.ShapeDtypeStruct((depth, D, DT_PAD), bf16)],
        compiler_params=_cparams(("parallel", "parallel")),
        name="w_in_split",
    )(w_t)


def _rows(v):
    return v.reshape(v.shape[0], 1, -1).astype(f32)


def _s5_block_diag(w2d, row_group, col_group, dtype):
    tiled = jnp.tile(w2d, (1, 1, S5_GROUPS))
    r = jnp.arange(tiled.shape[1]) // row_group
    c = jnp.arange(tiled.shape[2]) // col_group
    return jnp.where((r[:, None] == c[None, :])[None], tiled, 0.0).astype(dtype)


def _tile_tm(T):
    for tm in (512, 256, 128):
        if T % tm == 0:
            return tm
    raise ValueError("sequence length must be a multiple of 128")


def kernel(x, norm_mix, w_in, rwkv_mu, rwkv_w0, rwkv_w_up, rwkv_a0, rwkv_a_up, rwkv_g_up, rwkv_k_k, rwkv_k_a, rwkv_r_k, rwkv_ln_g, rwkv_ln_b, proj_a, ssd_conv_w, ssd_conv_b, ssd_dt_bias, ssd_a_log, ssd_d, ssd_norm_g, proj_b, s5_lam_re, s5_lam_im, s5_log_step, s5_b_re, s5_b_im, s5_c_re, s5_c_im, s5_d, s5_glu_w, s5_glu_b, proj_c, w_out, norm_ffn, ffn_w_in, ffn_w_out, final_norm):
    B, T, D = x.shape
    depth = w_in.shape[0]
    rw_nc = RW_CHUNKS_PER_STEP
    assert D == D_MODEL and T % SSD_CHUNK == 0 and B % 8 == 0 and T % (rw_nc * RW_CHUNK) == 0
    tm = _tile_tm(T)
    hid = jnp.arange(2 * RW_N) // RW_N
    head_ones = (hid[:, None] == hid[None, :]).astype(bf16)
    c_idx = jnp.arange(rw_nc * RW_CHUNK)
    tril_rw = ((c_idx[:, None] >= c_idx[None, :])
               & (c_idx[:, None] // RW_CHUNK == c_idx[None, :] // RW_CHUNK)).astype(bf16)
    l_idx = jnp.arange(tm)
    tril_ssd = ((l_idx[:, None] >= l_idx[None, :])
                & (l_idx[:, None] // SSD_CHUNK == l_idx[None, :] // SSD_CHUNK)).astype(bf16)

    g_mix = _rows(norm_mix)
    assert w_in.shape[2] == W_IN_COLS
    w_a, w_ug, w_dt = _win_split(w_in, 256)
    zr = jnp.zeros((depth, RW_DECAY_RANK, RW_W), f32)
    w_wa = jnp.concatenate([jnp.concatenate([rwkv_w_up, zr], axis=2),
                            jnp.concatenate([zr, rwkv_a_up], axis=2)], axis=1).astype(bf16)
    rw_layer = (_rows(rwkv_mu), jnp.concatenate([_rows(rwkv_w0), _rows(rwkv_a0)], axis=2), w_wa,
                rwkv_g_up.astype(bf16), _rows(rwkv_k_k), _rows(rwkv_k_a), _rows(rwkv_r_k),
                _rows(rwkv_ln_g), _rows(rwkv_ln_b))
    ssd_layer = (
        jnp.concatenate([ssd_conv_w, jnp.zeros((depth, 8 - SSD_CONV, SSD_XBC), f32)], axis=1),
        _rows(ssd_conv_b),
        jnp.concatenate([_rows(ssd_dt_bias), jnp.zeros((depth, 1, DT_PAD - SSD_HEADS), f32)], axis=2),
        jnp.concatenate([_rows(ssd_a_log), jnp.zeros((depth, 1, DT_PAD - SSD_HEADS), f32)], axis=2),
        _rows(jnp.repeat(ssd_d, SSD_P, axis=1)),
        _rows(ssd_norm_g))
    b_rows = lambda b: jnp.swapaxes(b, 2, 3).reshape(depth, S5_W, S5_STATE)
    c_rows = lambda c: jnp.swapaxes(c, 2, 3).reshape(depth, S5_SN, S5_GROUP)
    nlt = S5_SN // LANES
    b_tiles = lambda b: _s5_block_diag(b_rows(b), S5_GROUP, S5_STATE, bf16).reshape(depth, S5_W, nlt, LANES)
    w_b = jnp.stack([b_tiles(s5_b_re), b_tiles(s5_b_im)],
                    axis=3).reshape(depth, S5_W, 2 * S5_SN)
    s5_layer = (
        _rows(s5_lam_re), _rows(s5_lam_im), _rows(jnp.repeat(s5_log_step, S5_STATE, axis=1)),
        w_b,
        _s5_block_diag(c_rows(s5_c_re), S5_STATE, S5_GROUP, f32),
        _s5_block_diag(c_rows(s5_c_im), S5_STATE, S5_GROUP, f32),
        _rows(s5_d), s5_glu_w.astype(bf16), _rows(s5_glu_b))
    merge_layer = (proj_a.astype(bf16), proj_b.astype(bf16), proj_c.astype(bf16), w_out.astype(bf16))
    ffn_layer = (_rows(norm_ffn), ffn_w_in.astype(bf16), ffn_w_out.astype(bf16))
    fg = final_norm.reshape(1, D).astype(f32)
    s5_tt = 32 if T % 32 == 0 else 8

    for l in range(depth):
        p, z, xbc, u, dt = _inproj(x, g_mix, w_a, w_ug, w_dt, l, tm)
        oa = _rwkv(p, rw_layer, (head_ones, tril_rw), l, rw_nc)
        oc = _s5(u, s5_layer, l, s5_tt)
        x = _merge(x, oa, oc, z, xbc, dt, g_mix, w_ug, merge_layer, ssd_layer, (tril_ssd,), l, tm)
        x = _ffn(x, ffn_layer, fg, l, tm, final=(l == depth - 1))
    return x
```
